```python
import jax, jax.numpy as jnp
from jax import lax
import numpy as np

D_MODEL = 1024
BATCH = 2
SEQ = 16384
DEPTH = 4

RET_HEADS = 4
RET_DK = 128
RET_DV = 256
RET_CHUNK = 128
RET_QK = RET_HEADS * RET_DK
RET_VW = RET_HEADS * RET_DV
GLA_HEADS = 4
GLA_DK = 128
GLA_DV = 256
GLA_RANK = 16
GLA_TAU = 16.0
GLA_CHUNK = 64
GLA_QK = GLA_HEADS * GLA_DK
GLA_VW = GLA_HEADS * GLA_DV
SSD_INNER = 2 * D_MODEL
SSD_HEADDIM = 64
SSD_HEADS = SSD_INNER // SSD_HEADDIM
SSD_GROUPS = 4
SSD_STATE = 128
SSD_CONV = 4
SSD_CHUNK = 128
SSD_CONV_DIM = SSD_INNER + 2 * SSD_GROUPS * SSD_STATE
MOE_GROUPS = 4
MOE_PER_GROUP = 8
MOE_EXPERTS = MOE_GROUPS * MOE_PER_GROUP
MOE_TOPK = 2
MOE_FF = 512
MOE_BLOCK = 256

ROPE_BASE = 10000.0
EPS = 1e-6

SPLIT_SIZES = (RET_QK, RET_QK, RET_VW, RET_VW,
               GLA_QK, GLA_QK, GLA_VW, GLA_VW, GLA_RANK,
               SSD_INNER, SSD_CONV_DIM, SSD_HEADS,
               D_MODEL, D_MODEL, D_MODEL)
IN_WIDTH = sum(SPLIT_SIZES)
SPLIT_POINTS = tuple(int(v) for v in np.cumsum(SPLIT_SIZES)[:-1])

kernel_name = 'hybrid_ret_gla_ssd_hmoe'


def rmsnorm(x, g):
    xf = x.astype(jnp.float32)
    y = xf * lax.rsqrt(jnp.mean(xf * xf, axis=-1, keepdims=True) + EPS)
    return (y * g.astype(jnp.float32)).astype(x.dtype)


def head_layernorm(t, g):
    tf = t.astype(jnp.float32)
    mu = jnp.mean(tf, axis=-1, keepdims=True)
    c = tf - mu
    y = c * lax.rsqrt(jnp.mean(c * c, axis=-1, keepdims=True) + EPS)
    y = y * g.astype(jnp.float32).reshape(t.shape[-2:])
    return y.reshape(t.shape[:-2] + (-1,)).astype(t.dtype)


def head_rmsnorm(t, g):
    tf = t.astype(jnp.float32)
    y = tf * lax.rsqrt(jnp.mean(tf * tf, axis=-1, keepdims=True) + EPS)
    y = y * g.astype(jnp.float32).reshape(t.shape[-2:])
    return y.reshape(t.shape[:-2] + (-1,)).astype(t.dtype)


def rotary(t, pos):
    half = t.shape[-1] // 2
    inv = ROPE_BASE ** (-jnp.arange(half, dtype=jnp.float32) / half)
    ang = pos.astype(jnp.float32)[..., None] * inv
    cos = jnp.cos(ang)[:, :, None, :]
    sin = jnp.sin(ang)[:, :, None, :]
    t1 = t[..., :half].astype(jnp.float32)
    t2 = t[..., half:].astype(jnp.float32)
    return jnp.concatenate([t1 * cos - t2 * sin, t2 * cos + t1 * sin], axis=-1).astype(t.dtype)


def chunk_scan(decay, inc):
    def step(s, xs):
        d, a = xs
        return d * s + a, s
    _, prev = lax.scan(step, jnp.zeros_like(inc[0]), (decay, inc))
    return prev


def retention(q, k, v, log_gamma):
    B, S, H, dk = q.shape
    dv = v.shape[-1]
    C = RET_CHUNK
    n = S // C
    q = q.reshape(B, n, C, H, dk)
    k = k.reshape(B, n, C, H, dk)
    v = v.reshape(B, n, C, H, dv)
    idx = jnp.arange(C, dtype=jnp.float32)
    diff = idx[:, None] - idx[None, :]
    causal = diff >= 0
    dmat = jnp.where(causal[None], jnp.exp(jnp.where(causal, diff, 0.0)[None] * log_gamma[:, None, None]), 0.0)
    scores = jnp.einsum('bnihd,bnjhd->bnhij', q, k) * dmat.astype(q.dtype)
    o = jnp.einsum('bnhij,bnjhe->bnihe', scores, v)
    kdec = jnp.exp((C - 1 - idx)[:, None] * log_gamma[None]).astype(k.dtype)
    kv = jnp.einsum('bnjhd,bnjhe->nbhde', k * kdec[None, None, :, :, None], v)
    cdec = jnp.broadcast_to(jnp.exp(C * log_gamma)[None, None, :, None, None], (n, 1, H, 1, 1)).astype(kv.dtype)
    prev = chunk_scan(cdec, kv)
    qdec = jnp.exp((idx + 1)[:, None] * log_gamma[None]).astype(q.dtype)
    o = o + jnp.einsum('bnihd,nbhde->bnihe', q * qdec[None, None, :, :, None], prev)
    return o.reshape(B, S, H, dv)


def gla(q, k, v, log_a):
    B, S, H, dk = q.shape
    dv = v.shape[-1]
    C = GLA_CHUNK
    n = S // C
    q = q.reshape(B, n, C, H, dk)
    k = k.reshape(B, n, C, H, dk)
    v = v.reshape(B, n, C, H, dv)
    b = jnp.cumsum(log_a.astype(jnp.float32).reshape(B, n, C, H, dk), axis=2)
    b_ref = b[:, :, C // 2:C // 2 + 1]
    b_last = b[:, :, -1:]
    qi = q * jnp.exp(b - b_ref).astype(q.dtype)
    ki = k * jnp.exp(b_ref - b).astype(k.dtype)
    causal = jnp.tril(jnp.ones((C, C), dtype=bool))
    scores = jnp.where(causal, jnp.einsum('bnihd,bnjhd->bnhij', qi, ki), 0.0).astype(v.dtype)
    o = jnp.einsum('bnhij,bnjhe->bnihe', scores, v)
    ks = k * jnp.exp(b_last - b).astype(k.dtype)
    kv = jnp.einsum('bnjhd,bnjhe->nbhde', ks, v)
    dec = jnp.exp(b_last[:, :, 0]).transpose(1, 0, 2, 3)[..., None].astype(kv.dtype)
    prev = chunk_scan(dec, kv)
    o = o + jnp.einsum('bnihd,nbhde->bnihe', q * jnp.exp(b).astype(q.dtype), prev)
    return o.reshape(B, S, H, dv)


def ssd_scan(x, dt, a, bm, cm):
    B, S, H, P = x.shape
    G, N = bm.shape[2], bm.shape[3]
    R = H // G
    C = SSD_CHUNK
    n = S // C
    acs = jnp.cumsum((dt * a).reshape(B, n, C, H), axis=2)
    xdt = (x.astype(jnp.float32) * dt[..., None]).astype(x.dtype).reshape(B, n, C, G, R, P)
    bc = bm.reshape(B, n, C, G, N)
    cc = cm.reshape(B, n, C, G, N)
    at = acs.transpose(0, 1, 3, 2)
    mask = jnp.tril(jnp.ones((C, C), dtype=bool))
    seg = at[..., :, None] - at[..., None, :]
    lmat = jnp.exp(jnp.where(mask, seg, -jnp.inf)).reshape(B, n, G, R, C, C).astype(x.dtype)
    cb = jnp.einsum('bnigs,bnjgs->bngij', cc, bc)
    y = jnp.einsum('bngrij,bnjgrp->bnigrp', lmat * cb[:, :, :, None], xdt)
    dstates = jnp.exp(acs[:, :, -1:, :] - acs).reshape(B, n, C, G, R).astype(x.dtype)
    states = jnp.einsum('bnjgs,bnjgrp->nbgrps', bc, xdt * dstates[..., None])
    cdec = jnp.exp(acs[:, :, -1, :]).reshape(B, n, G, R).transpose(1, 0, 2, 3)[..., None, None].astype(states.dtype)
    prev = chunk_scan(cdec, states)
    dout = jnp.exp(acs).reshape(B, n, C, G, R).astype(x.dtype)
    y = y + jnp.einsum('bnigs,nbgrps->bnigrp', cc, prev) * dout[..., None]
    return y.reshape(B, S, H, P)


def causal_depthwise_conv(x, w, b):
    K, ch = w.shape
    y = lax.conv_general_dilated(x, w[:, None, :].astype(x.dtype), window_strides=(1,),
                                 padding=[(K - 1, 0)], dimension_numbers=('NWC', 'WIO', 'NWC'),
                                 feature_group_count=ch)
    return y + b.astype(x.dtype)


def hier_moe(h, w_rg, b_rg, w_re, b_re, w_g, w_u, w_d):
    T, D = h.shape
    E = w_g.shape[0]
    grp_logits = jnp.dot(h, w_rg).astype(jnp.float32) + b_rg.astype(jnp.float32)
    p_grp = jax.nn.softmax(grp_logits, axis=-1)
    g_sel = jnp.argmax(grp_logits, axis=-1).astype(jnp.int32)
    p_sel = jnp.take_along_axis(p_grp, g_sel[:, None], axis=-1)
    e_logits = (jnp.dot(h, w_re).astype(jnp.float32) + b_re.astype(jnp.float32)).reshape(T, MOE_GROUPS, MOE_PER_GROUP)
    in_grp = jnp.take_along_axis(e_logits, g_sel[:, None, None], axis=1)[:, 0]
    top_v, top_i = lax.top_k(in_grp, MOE_TOPK)
    gate = jax.nn.softmax(top_v, axis=-1) * p_sel
    eid = (g_sel[:, None] * MOE_PER_GROUP + top_i).reshape(-1).astype(jnp.int32)
    wts = gate.reshape(-1)
    A = T * MOE_TOPK
    s_eid, order = lax.sort((eid, jnp.arange(A, dtype=jnp.int32)), num_keys=1, is_stable=True)
    s_tok = order // MOE_TOPK
    s_w = wts[order]
    counts = jnp.bincount(eid, length=E).astype(jnp.int32)
    starts = jnp.cumsum(counts) - counts
    pcounts = (counts + MOE_BLOCK - 1) // MOE_BLOCK * MOE_BLOCK
    pends = jnp.cumsum(pcounts)
    pstarts = pends - pcounts
    dest = pstarts[s_eid] + jnp.arange(A, dtype=jnp.int32) - starts[s_eid]
    n_blocks = -(-A // MOE_BLOCK) + E
    P = n_blocks * MOE_BLOCK
    x_buf = jnp.zeros((P, D), h.dtype).at[dest].set(h[s_tok])
    t_buf = jnp.full((P,), T, jnp.int32).at[dest].set(s_tok)
    w_buf = jnp.zeros((P,), jnp.float32).at[dest].set(s_w)
    blk_e = jnp.minimum(jnp.searchsorted(pends, jnp.arange(n_blocks, dtype=jnp.int32) * MOE_BLOCK, side='right'), E - 1).astype(jnp.int32)

    def run_block(args):
        xb, e = args
        return jnp.dot(jax.nn.silu(jnp.dot(xb, w_g[e])) * jnp.dot(xb, w_u[e]), w_d[e])

    y = lax.map(run_block, (x_buf.reshape(n_blocks, MOE_BLOCK, D), blk_e)).reshape(P, D)
    y = (y.astype(jnp.float32) * w_buf[:, None]).astype(h.dtype)
    return jax.ops.segment_sum(y, t_buf, num_segments=T + 1)[:T]


def setup_inputs(seed: int = 0) -> dict:
    key = jax.random.key(seed)
    ks = jax.random.split(key, 32)
    f32 = jnp.float32
    L = DEPTH
    D = D_MODEL
    res_scale = (2.0 * DEPTH) ** -0.5

    def nrm(k, shape, scale):
        return jax.random.normal(k, shape, f32) * scale

    x = nrm(ks[0], (BATCH, SEQ, D), 1.0)
    positions = jnp.arange(SEQ, dtype=jnp.int32)[None, :] + jax.random.randint(ks[1], (BATCH, 1), 0, 4096, dtype=jnp.int32)
    g_mix = 1.0 + nrm(ks[2], (L, D), 0.02)
    w_in = nrm(ks[3], (L, D, IN_WIDTH), D ** -0.5)
    ret_norm = 1.0 + nrm(ks[4], (L, RET_VW), 0.02)
    w_ret_out = nrm(ks[5], (L, RET_VW, D), RET_VW ** -0.5)
    gla_w_a2 = nrm(ks[6], (L, GLA_RANK, GLA_QK), GLA_RANK ** -0.5)
    gla_b_a = nrm(ks[7], (L, GLA_QK), 0.02)
    gla_norm = 1.0 + nrm(ks[8], (L, GLA_VW), 0.02)
    w_gla_out = nrm(ks[9], (L, GLA_VW, D), GLA_VW ** -0.5)
    ssd_conv_w = nrm(ks[10], (L, SSD_CONV, SSD_CONV_DIM), SSD_CONV ** -0.5)
    ssd_conv_b = nrm(ks[11], (L, SSD_CONV_DIM), 0.02)
    dt0 = jnp.exp(jax.random.uniform(ks[12], (L, SSD_HEADS), f32, float(np.log(1e-3)), float(np.log(1e-1))))
    ssd_dt_bias = dt0 + jnp.log(-jnp.expm1(-dt0))
    ssd_a_log = jnp.log(jax.random.uniform(ks[13], (L, SSD_HEADS), f32, 1.0, 16.0))
    ssd_d = 1.0 + nrm(ks[14], (L, SSD_HEADS), 0.02)
    ssd_norm = 1.0 + nrm(ks[15], (L, SSD_INNER), 0.02)
    w_ssd_out = nrm(ks[16], (L, SSD_INNER, D), SSD_INNER ** -0.5)
    w_o = nrm(ks[17], (L, D, D), D ** -0.5 * res_scale)
    g_ffn = 1.0 + nrm(ks[18], (L, D), 0.02)
    w_rg = nrm(ks[19], (L, D, MOE_GROUPS), D ** -0.5)
    b_rg = nrm(ks[20], (L, MOE_GROUPS), 0.01)
    w_re = nrm(ks[21], (L, D, MOE_EXPERTS), D ** -0.5)
    b_re = nrm(ks[22], (L, MOE_EXPERTS), 0.01)
    w_exp_gate = nrm(ks[23], (L, MOE_EXPERTS, D, MOE_FF), D ** -0.5)
    w_exp_up = nrm(ks[24], (L, MOE_EXPERTS, D, MOE_FF), D ** -0.5)
    w_exp_down = nrm(ks[25], (L, MOE_EXPERTS, MOE_FF, D), MOE_FF ** -0.5 * res_scale)
    g_final = 1.0 + nrm(ks[26], (D,), 0.02)
    return {'x': x, 'positions': positions, 'g_mix': g_mix, 'w_in': w_in,
            'ret_norm': ret_norm, 'w_ret_out': w_ret_out,
            'gla_w_a2': gla_w_a2, 'gla_b_a': gla_b_a, 'gla_norm': gla_norm, 'w_gla_out': w_gla_out,
            'ssd_conv_w': ssd_conv_w, 'ssd_conv_b': ssd_conv_b, 'ssd_dt_bias': ssd_dt_bias,
            'ssd_a_log': ssd_a_log, 'ssd_d': ssd_d, 'ssd_norm': ssd_norm, 'w_ssd_out': w_ssd_out,
            'w_o': w_o, 'g_ffn': g_ffn, 'w_rg': w_rg, 'b_rg': b_rg, 'w_re': w_re, 'b_re': b_re,
            'w_exp_gate': w_exp_gate, 'w_exp_up': w_exp_up, 'w_exp_down': w_exp_down,
            'g_final': g_final}


def reference(x, positions, g_mix, w_in, ret_norm, w_ret_out, gla_w_a2, gla_b_a, gla_norm, w_gla_out,
              ssd_conv_w, ssd_conv_b, ssd_dt_bias, ssd_a_log, ssd_d, ssd_norm, w_ssd_out, w_o,
              g_ffn, w_rg, b_rg, w_re, b_re, w_exp_gate, w_exp_up, w_exp_down, g_final):
    B, S, D = x.shape
    log_gamma = jnp.log1p(-jnp.exp2(-5.0 - jnp.arange(RET_HEADS, dtype=jnp.float32)))
    for l in range(DEPTH):
        h = rmsnorm(x, g_mix[l])
        u = jnp.dot(h, w_in[l])
        (rq, rk, rv, rg, gq, gk, gv, gr, ga, sz, sxbc, sdt, gt_a, gt_b, gt_c) = jnp.split(u, SPLIT_POINTS, axis=-1)

        q = rotary(rq.reshape(B, S, RET_HEADS, RET_DK), positions)
        k = rotary(rk.reshape(B, S, RET_HEADS, RET_DK), positions) * (RET_DK ** -0.5)
        o = retention(q, k, rv.reshape(B, S, RET_HEADS, RET_DV), log_gamma)
        o = head_layernorm(o, ret_norm[l]) * jax.nn.silu(rg)
        y_a = jnp.dot(o, w_ret_out[l])

        log_a = jax.nn.log_sigmoid((jnp.dot(ga, gla_w_a2[l]) + gla_b_a[l]).astype(jnp.float32)) / GLA_TAU
        o = gla(gq.reshape(B, S, GLA_HEADS, GLA_DK) * (GLA_DK ** -0.5),
                gk.reshape(B, S, GLA_HEADS, GLA_DK),
                gv.reshape(B, S, GLA_HEADS, GLA_DV),
                log_a.reshape(B, S, GLA_HEADS, GLA_DK))
        o = head_rmsnorm(o, gla_norm[l]) * jax.nn.silu(gr)
        y_b = jnp.dot(o, w_gla_out[l])

        xbc = jax.nn.silu(causal_depthwise_conv(sxbc, ssd_conv_w[l], ssd_conv_b[l]))
        xs, bm, cm = jnp.split(xbc, [SSD_INNER, SSD_INNER + SSD_GROUPS * SSD_STATE], axis=-1)
        dt = jax.nn.softplus(sdt.astype(jnp.float32) + ssd_dt_bias[l].astype(jnp.float32))
        a = -jnp.exp(ssd_a_log[l].astype(jnp.float32))
        xh = xs.reshape(B, S, SSD_HEADS, SSD_HEADDIM)
        y = ssd_scan(xh, dt, a, bm.reshape(B, S, SSD_GROUPS, SSD_STATE), cm.reshape(B, S, SSD_GROUPS, SSD_STATE))
        y = y + xh * ssd_d[l][:, None]
        y = y.reshape(B, S, SSD_INNER) * jax.nn.silu(sz)
        y = head_rmsnorm(y.reshape(B, S, SSD_GROUPS, SSD_INNER // SSD_GROUPS), ssd_norm[l])
        y_c = jnp.dot(y, w_ssd_out[l])

        merged = jax.nn.sigmoid(gt_a) * y_a + jax.nn.sigmoid(gt_b) * y_b + jax.nn.sigmoid(gt_c) * y_c
        x = x + jnp.dot(merged, w_o[l])

        hf = rmsnorm(x, g_ffn[l]).reshape(B * S, D)
        x = x + hier_moe(hf, w_rg[l], b_rg[l], w_re[l], b_re[l],
                         w_exp_gate[l], w_exp_up[l], w_exp_down[l]).reshape(B, S, D)
    return rmsnorm(x, g_final)
```

```python
import functools
import math

import jax
import jax.numpy as jnp
from jax import lax
from jax.experimental import pallas as pl
from jax.experimental.pallas import tpu as pltpu

F32 = jnp.float32
BF16 = jnp.bfloat16
I32 = jnp.int32

D_MODEL = 1024
RET_HEADS, RET_DK, RET_DV, RET_CHUNK = 4, 128, 256, 128
GLA_HEADS, GLA_DK, GLA_DV, GLA_RANK, GLA_TAU, GLA_CHUNK = 4, 128, 256, 16, 16.0, 64
SSD_INNER, SSD_HEADDIM, SSD_GROUPS, SSD_STATE, SSD_CONV, SSD_CHUNK = 2048, 64, 4, 128, 4, 128
SSD_HEADS = SSD_INNER // SSD_HEADDIM
SSD_HEADS_PER_GROUP = SSD_HEADS // SSD_GROUPS
SSD_CONV_DIM = SSD_INNER + 2 * SSD_GROUPS * SSD_STATE
MOE_GROUPS, MOE_PER_GROUP, MOE_TOPK, MOE_FF = 4, 8, 2, 512
MOE_EXPERTS = MOE_GROUPS * MOE_PER_GROUP
ROPE_BASE = 10000.0
EPS = 1e-6
LANES = 128

RET_QK = RET_HEADS * RET_DK
RET_VW = RET_HEADS * RET_DV
GLA_QK = GLA_HEADS * GLA_DK
GLA_VW = GLA_HEADS * GLA_DV
SPLIT_SIZES = (RET_QK, RET_QK, RET_VW, RET_VW,
               GLA_QK, GLA_QK, GLA_VW, GLA_VW, GLA_RANK,
               SSD_INNER, SSD_CONV_DIM, SSD_HEADS,
               D_MODEL, D_MODEL, D_MODEL)
_OFFS = [0]
for _s in SPLIT_SIZES:
    _OFFS.append(_OFFS[-1] + _s)
(O_RQ, O_RK, O_RV, O_RG, O_GQ, O_GK, O_GV, O_GR, O_GA, O_SZ, O_SXBC, O_SDT, O_GTA, O_GTB, O_GTC, _) = _OFFS

TS_A = 512
TS_B = 512
TS_C = 256
TM_ROUTER = 512
TM_MOVE = 256
BM_EXPERT = 512
CONV_PAD = 8
VMEM_LIMIT = 56 * 1024 * 1024


def _dot(a, b):
    return jnp.dot(a, b, preferred_element_type=F32)


def _dot_nt(a, b):
    return lax.dot_general(a, b, (((1,), (1,)), ((), ())), preferred_element_type=F32)


def _dot_tn(a, b):
    return lax.dot_general(a, b, (((0,), (0,)), ((), ())), preferred_element_type=F32)


def _dot_hi(a, b):
    return jnp.dot(a, b, preferred_element_type=F32, precision=lax.Precision.HIGHEST)


def _dot_tn_hi(a, b):
    return lax.dot_general(a, b, (((0,), (0,)), ((), ())), preferred_element_type=F32,
                           precision=lax.Precision.HIGHEST)


def _dot_nt_hi(a, b):
    return lax.dot_general(a, b, (((1,), (1,)), ((), ())), preferred_element_type=F32,
                           precision=lax.Precision.HIGHEST)


def _rmsnorm(x, g):
    return x * lax.rsqrt(jnp.mean(x * x, axis=-1, keepdims=True) + EPS) * g


def _silu(x):
    return x * jax.nn.sigmoid(x)


def _const_spec(shape):
    nd = len(shape)
    return pl.BlockSpec(shape, lambda *_: (0,) * nd, pipeline_mode=pl.Buffered(1))


def _seq_spec(ts, width):
    return pl.BlockSpec((None, ts, width), lambda b, s: (b, s, 0))


def _mixer_a_kernel(x_ref, pos_ref, inv_ref, g_ref, w_ref, nrm_ref, wout_ref, o_ref,
                    h_scr, o_scr, st_ref):
    C = RET_CHUNK
    ts = x_ref.shape[0]

    @pl.when(pl.program_id(1) == 0)
    def _():
        st_ref[...] = jnp.zeros_like(st_ref)

    h_scr[...] = _rmsnorm(x_ref[...], g_ref[...]).astype(BF16)
    h = h_scr[...]

    ang = pos_ref[...] * inv_ref[...]
    lane = lax.broadcasted_iota(I32, (1, LANES), 1)
    cos2 = jnp.cos(ang)
    sin2 = jnp.sin(ang) * jnp.where(lane < RET_DK // 2, -1.0, 1.0)

    def rot(t):
        return t * cos2 + pltpu.roll(t, RET_DK // 2, 1) * sin2

    ii = lax.broadcasted_iota(I32, (C, C), 0)
    jj = lax.broadcasted_iota(I32, (C, C), 1)
    dif = (ii - jj).astype(F32)
    row = lax.broadcasted_iota(I32, (C, 1), 0).astype(F32)

    for hd in range(RET_HEADS):
        lg = math.log1p(-(2.0 ** (-5.0 - hd)))
        dmat = jnp.where(dif >= 0, jnp.exp(jnp.maximum(dif, 0.0) * lg), 0.0)
        qdec = jnp.exp((row + 1.0) * lg)
        kdec = jnp.exp((C - 1.0 - row) * lg)
        cdec = math.exp(C * lg)
        q = rot(_dot(h, w_ref[:, hd * RET_DK:(hd + 1) * RET_DK]))
        k = rot(_dot(h, w_ref[:, RET_QK + hd * RET_DK:RET_QK + (hd + 1) * RET_DK])) * (RET_DK ** -0.5)
        v = _dot(h, w_ref[:, 2 * RET_QK + hd * RET_DV:2 * RET_QK + (hd + 1) * RET_DV])
        g = _dot(h, w_ref[:, 2 * RET_QK + RET_VW + hd * RET_DV:2 * RET_QK + RET_VW + (hd + 1) * RET_DV])
        nrm = nrm_ref[:, hd * RET_DV:(hd + 1) * RET_DV]
        for c in range(ts // C):
            sl = slice(c * C, (c + 1) * C)
            qc, kc = q[sl], k[sl]
            vcb = v[sl].astype(BF16)
            sc = _dot_nt(qc.astype(BF16), kc.astype(BF16)) * dmat
            st = st_ref[hd]
            o = _dot(sc.astype(BF16), vcb) + _dot((qc * qdec).astype(BF16), st.astype(BF16))
            st_ref[hd] = cdec * st + _dot_tn((kc * kdec).astype(BF16), vcb)
            cen = o - jnp.mean(o, axis=-1, keepdims=True)
            y = cen * lax.rsqrt(jnp.mean(cen * cen, axis=-1, keepdims=True) + EPS) * nrm
            o_scr[sl, hd * RET_DV:(hd + 1) * RET_DV] = (y * _silu(g[sl])).astype(BF16)

    ya = _dot(o_scr[...], wout_ref[...])
    gate = jax.nn.sigmoid(_dot(h, w_ref[:, 2 * RET_QK + 2 * RET_VW:]))
    o_ref[...] = ya * gate


def _mixer_a(x, posf, inv2, g, w, nrm, wout):
    B, S, D = x.shape
    ts = min(TS_A, S)
    return pl.pallas_call(
        _mixer_a_kernel,
        grid=(B, S // ts),
        in_specs=[_seq_spec(ts, D), _seq_spec(ts, 1), _const_spec(inv2.shape), _const_spec(g.shape),
                  _const_spec(w.shape), _const_spec(nrm.shape), _const_spec(wout.shape)],
        out_specs=_seq_spec(ts, D),
        out_shape=jax.ShapeDtypeStruct((B, S, D), F32),
        scratch_shapes=[pltpu.VMEM((ts, D), BF16), pltpu.VMEM((ts, RET_VW), BF16),
                        pltpu.VMEM((RET_HEADS, RET_DK, RET_DV), F32)],
        compiler_params=pltpu.CompilerParams(dimension_semantics=("arbitrary", "arbitrary"),
                                             vmem_limit_bytes=VMEM_LIMIT),
        name="mixer_a",
    )(x, posf, inv2, g, w, nrm, wout)


def _mixer_b_kernel(x_ref, m_ref, g_ref, w_ref, wa2_ref, ba_ref, nrm_ref, wout_ref, o_ref,
                    h_scr, la_scr, o_scr, st_ref):
    C = GLA_CHUNK
    ts = x_ref.shape[0]

    @pl.when(pl.program_id(1) == 0)
    def _():
        st_ref[...] = jnp.zeros_like(st_ref)

    h_scr[...] = _rmsnorm(x_ref[...], g_ref[...]).astype(BF16)
    h = h_scr[...]

    o_ga = 2 * GLA_QK + 2 * GLA_VW + D_MODEL
    ga = _dot(h, w_ref[:, o_ga:o_ga + LANES])
    z = _dot(ga.astype(BF16), wa2_ref[...]) + ba_ref[...]
    la_scr[...] = (jnp.minimum(z, 0.0) - jnp.log1p(jnp.exp(-jnp.abs(z)))) * (1.0 / GLA_TAU)

    ii = lax.broadcasted_iota(I32, (C, C), 0)
    jj = lax.broadcasted_iota(I32, (C, C), 1)
    causal = ii >= jj
    tril = causal.astype(F32)

    q_all = _dot(h, w_ref[:, 0:GLA_QK]) * (GLA_DK ** -0.5)
    k_all = _dot(h, w_ref[:, GLA_QK:2 * GLA_QK])

    for hd in range(GLA_HEADS):
        v = _dot(h, w_ref[:, 2 * GLA_QK + hd * GLA_DV:2 * GLA_QK + (hd + 1) * GLA_DV])
        gr = _dot(h, w_ref[:, 2 * GLA_QK + GLA_VW + hd * GLA_DV:2 * GLA_QK + GLA_VW + (hd + 1) * GLA_DV])
        nrm = nrm_ref[:, hd * GLA_DV:(hd + 1) * GLA_DV]
        hs = slice(hd * GLA_DK, (hd + 1) * GLA_DK)
        for c in range(ts // C):
            sl = slice(c * C, (c + 1) * C)
            b = _dot_hi(tril, la_scr[sl, hs])
            b_ref = b[C // 2:C // 2 + 1]
            b_last = b[C - 1:C]
            qc, kc = q_all[sl, hs], k_all[sl, hs]
            vcb = v[sl].astype(BF16)
            qi = (qc * jnp.exp(b - b_ref)).astype(BF16)
            ki = (kc * jnp.exp(b_ref - b)).astype(BF16)
            sc = jnp.where(causal, _dot_nt(qi, ki), 0.0).astype(BF16)
            st = st_ref[hd]
            o = _dot(sc, vcb) + _dot_nt((qc * jnp.exp(b)).astype(BF16), st.astype(BF16))
            ks = (kc * jnp.exp(b_last - b)).astype(BF16)
            st_ref[hd] = jnp.exp(b_last) * st + _dot_tn(vcb, ks)
            y = o * lax.rsqrt(jnp.mean(o * o, axis=-1, keepdims=True) + EPS) * nrm
            o_scr[sl, hd * GLA_DV:(hd + 1) * GLA_DV] = (y * _silu(gr[sl])).astype(BF16)

    yb = _dot(o_scr[...], wout_ref[...])
    gate = jax.nn.sigmoid(_dot(h, w_ref[:, 2 * GLA_QK + 2 * GLA_VW:2 * GLA_QK + 2 * GLA_VW + D_MODEL]))
    o_ref[...] = m_ref[...] + yb * gate


def _mixer_b(x, m, g, w, wa2, ba, nrm, wout):
    B, S, D = x.shape
    ts = min(TS_B, S)
    return pl.pallas_call(
        _mixer_b_kernel,
        grid=(B, S // ts),
        in_specs=[_seq_spec(ts, D), _seq_spec(ts, D), _const_spec(g.shape), _const_spec(w.shape),
                  _const_spec(wa2.shape), _const_spec(ba.shape), _const_spec(nrm.shape),
                  _const_spec(wout.shape)],
        out_specs=_seq_spec(ts, D),
        out_shape=jax.ShapeDtypeStruct((B, S, D), F32),
        scratch_shapes=[pltpu.VMEM((ts, D), BF16), pltpu.VMEM((ts, GLA_QK), F32),
                        pltpu.VMEM((ts, GLA_VW), BF16),
                        pltpu.VMEM((GLA_HEADS, GLA_DV, GLA_DK), F32)],
        compiler_params=pltpu.CompilerParams(dimension_semantics=("arbitrary", "arbitrary"),
                                             vmem_limit_bytes=VMEM_LIMIT),
        name="mixer_b",
    )(x, m, g, w, wa2, ba, nrm, wout)


def _mixer_c_kernel(x_ref, m_ref, g_ref, w_ref, cw_ref, cb_ref, dtb_ref, a_ref, dx_ref, nrm_ref,
                    wout_ref, wo_ref, o_ref,
                    h_scr, cbuf, xs_scr, bc_scr, z_scr, dt_scr, y_scr, st_ref):
    C = SSD_CHUNK
    N = SSD_STATE
    P = SSD_HEADDIM
    ts = x_ref.shape[0]
    GN = SSD_GROUPS * N

    @pl.when(pl.program_id(1) == 0)
    def _():
        st_ref[...] = jnp.zeros_like(st_ref)
        cbuf[0:CONV_PAD, :] = jnp.zeros((CONV_PAD, SSD_CONV_DIM), F32)

    h_scr[...] = _rmsnorm(x_ref[...], g_ref[...]).astype(BF16)
    h = h_scr[...]

    z_scr[...] = _silu(_dot(h, w_ref[:, 0:SSD_INNER]))
    o_gt = SSD_INNER + SSD_CONV_DIM
    o_dt = o_gt + D_MODEL
    dt_scr[...] = jax.nn.softplus(_dot(h, w_ref[:, o_dt:o_dt + LANES]) + dtb_ref[...])

    CB = 512
    for cblk in range(SSD_CONV_DIM // CB):
        cs = slice(cblk * CB, (cblk + 1) * CB)
        cbuf[CONV_PAD:CONV_PAD + ts, cs] = _dot(h, w_ref[:, SSD_INNER + cblk * CB:SSD_INNER + (cblk + 1) * CB])
        acc = cb_ref[:, cs]
        for j in range(SSD_CONV):
            off = CONV_PAD - (SSD_CONV - 1) + j
            acc = acc + cbuf[off:off + ts, cs] * cw_ref[j:j + 1, cs]
        act = _silu(acc)
        if cblk * CB < SSD_INNER:
            xs_scr[:, cs] = act
        else:
            bc_scr[:, cblk * CB - SSD_INNER:(cblk + 1) * CB - SSD_INNER] = act.astype(BF16)
        cbuf[0:CONV_PAD, cs] = cbuf[ts:ts + CONV_PAD, cs]

    ii = lax.broadcasted_iota(I32, (C, C), 0)
    jj = lax.broadcasted_iota(I32, (C, C), 1)
    causal = ii >= jj
    tril = causal.astype(F32)
    triu = (ii <= jj).astype(F32)
    a_row = a_ref[...]

    def chunk_body(c, carry):
        r0 = pl.multiple_of(c * C, C)
        rs = pl.ds(r0, C)
        dt = dt_scr[rs, :]
        dta = dt * a_row
        acs = _dot_hi(tril, dta)
        acs_t = _dot_tn_hi(dta, triu)
        dt_t = dt.T
        last_row = acs[C - 1:C, :]
        w1_t = dt_t * jnp.exp(acs_t[:, C - 1:C] - acs_t)
        for g in range(SSD_GROUPS):
            bcg = bc_scr[rs, g * N:(g + 1) * N]
            ccg = bc_scr[rs, GN + g * N:GN + (g + 1) * N]
            cb = _dot_nt(ccg, bcg)
            ccf = ccg.astype(F32)
            bct = bcg.astype(F32).T
            for r in range(SSD_HEADS_PER_GROUP):
                hh = g * SSD_HEADS_PER_GROUP + r
                col = jnp.broadcast_to(acs[:, hh:hh + 1], (C, C))
                seg = col - acs_t[hh:hh + 1, :]
                lmat = jnp.where(causal, jnp.exp(jnp.where(causal, seg, 0.0)), 0.0)
                a1 = (lmat * cb * dt_t[hh:hh + 1, :]).astype(BF16)
                a2 = (jnp.exp(col) * ccf).astype(BF16)
                xh = xs_scr[rs, hh * P:(hh + 1) * P]
                xhb = xh.astype(BF16)
                st = st_ref[hh]
                y = _dot(a1, xhb) + _dot(a2, st.astype(BF16))
                new = _dot((bct * w1_t[hh:hh + 1, :]).astype(BF16), xhb)
                st_ref[hh] = jnp.exp(last_row[:, hh:hh + 1]) * st + new
                y_scr[rs, hh * P:(hh + 1) * P] = y
        return carry

    lax.fori_loop(0, ts // C, chunk_body, 0)

    GW = SSD_INNER // SSD_GROUPS
    for g in range(SSD_GROUPS):
        gs = slice(g * GW, (g + 1) * GW)
        y = (y_scr[:, gs] + xs_scr[:, gs] * dx_ref[:, gs]) * z_scr[:, gs]
        y = y * lax.rsqrt(jnp.mean(y * y, axis=-1, keepdims=True) + EPS) * nrm_ref[:, gs]
        y_scr[:, gs] = y
    yc = _dot(y_scr[...].astype(BF16), wout_ref[...])
    gate = jax.nn.sigmoid(_dot(h, w_ref[:, o_gt:o_gt + D_MODEL]))
    merged = m_ref[...] + yc * gate
    o_ref[...] = x_ref[...] + _dot(merged.astype(BF16), wo_ref[...])


def _mixer_c(x, m, g, w, cw, cb, dtb, a_row, dx, nrm, wout, wo):
    B, S, D = x.shape
    ts = min(TS_C, S)
    consts = (g, w, cw, cb, dtb, a_row, dx, nrm, wout, wo)
    return pl.pallas_call(
        _mixer_c_kernel,
        grid=(B, S // ts),
        in_specs=[_seq_spec(ts, D), _seq_spec(ts, D)] + [_const_spec(c.shape) for c in consts],
        out_specs=_seq_spec(ts, D),
        out_shape=jax.ShapeDtypeStruct((B, S, D), F32),
        scratch_shapes=[pltpu.VMEM((ts, D), BF16),
                        pltpu.VMEM((ts + CONV_PAD, SSD_CONV_DIM), F32),
                        pltpu.VMEM((ts, SSD_INNER), F32),
                        pltpu.VMEM((ts, 2 * SSD_GROUPS * SSD_STATE), BF16),
                        pltpu.VMEM((ts, SSD_INNER), F32),
                        pltpu.VMEM((ts, LANES), F32),
                        pltpu.VMEM((ts, SSD_INNER), F32),
                        pltpu.VMEM((SSD_HEADS, SSD_STATE, SSD_HEADDIM), F32)],
        compiler_params=pltpu.CompilerParams(dimension_semantics=("arbitrary", "arbitrary"),
                                             vmem_limit_bytes=VMEM_LIMIT),
        name="mixer_c",
    )(x, m, *consts)


ROUTER_ROWS = 8 + MOE_EXPERTS


def _router_kernel(x_ref, g_ref, wr_ref, br_ref, hf_ref, mi_ref, mf_ref, cnt_ref, cnt_scr):
    tm = x_ref.shape[0]
    E = MOE_EXPERTS
    NEG = -jnp.inf

    @pl.when(pl.program_id(0) == 0)
    def _():
        cnt_scr[...] = jnp.zeros_like(cnt_scr)

    hf = _rmsnorm(x_ref[...], g_ref[...])
    hf_ref[...] = hf
    logits = _dot_nt_hi(wr_ref[...], hf) + br_ref[...]

    sub8 = lax.broadcasted_iota(I32, (8, tm), 0)
    gl = jnp.where(sub8 < MOE_GROUPS, logits[0:8], NEG)
    gmax = jnp.max(gl, axis=0, keepdims=True)
    g_sel = jnp.min(jnp.where(gl == gmax, sub8, 8), axis=0, keepdims=True)
    p_sel = 1.0 / jnp.sum(jnp.exp(gl - gmax), axis=0, keepdims=True)

    el = jnp.zeros((MOE_PER_GROUP, tm), F32)
    for g in range(MOE_GROUPS):
        el = jnp.where(g_sel == g, logits[8 + g * MOE_PER_GROUP:8 + (g + 1) * MOE_PER_GROUP], el)
    m1 = jnp.max(el, axis=0, keepdims=True)
    i1 = jnp.min(jnp.where(el == m1, sub8, 8), axis=0, keepdims=True)
    el2 = jnp.where(sub8 == i1, NEG, el)
    m2 = jnp.max(el2, axis=0, keepdims=True)
    i2 = jnp.min(jnp.where(el2 == m2, sub8, 8), axis=0, keepdims=True)
    e2 = jnp.exp(m2 - m1)
    w0 = p_sel / (1.0 + e2)
    w1 = p_sel * e2 / (1.0 + e2)
    eid0 = g_sel * MOE_PER_GROUP + i1
    eid1 = g_sel * MOE_PER_GROUP + i2

    subE = lax.broadcasted_iota(I32, (E, tm), 0)
    oh0 = subE == eid0
    oh1 = subE == eid1
    oh = (oh0 | oh1).astype(BF16)
    t_r = lax.broadcasted_iota(I32, (tm, tm), 0)
    t_c = lax.broadcasted_iota(I32, (tm, tm), 1)
    before = (t_r < t_c).astype(BF16)
    tot = _dot(oh, before) + cnt_scr[:, 0:1]
    rank0 = jnp.sum(jnp.where(oh0, tot, 0.0), axis=0, keepdims=True)
    rank1 = jnp.sum(jnp.where(oh1, tot, 0.0), axis=0, keepdims=True)
    cnt_new = cnt_scr[...] + jnp.sum(oh.astype(F32), axis=1, keepdims=True)
    cnt_scr[...] = cnt_new
    cnt_ref[...] = cnt_new.astype(I32)

    mi_ref[...] = jnp.zeros_like(mi_ref)
    mi_ref[0:1, :] = eid0
    mi_ref[1:2, :] = eid1
    mi_ref[2:3, :] = rank0.astype(I32)
    mi_ref[3:4, :] = rank1.astype(I32)
    mf_ref[...] = jnp.zeros_like(mf_ref)
    mf_ref[0:1, :] = w0
    mf_ref[1:2, :] = w1


def _router(x2, g, wr, br):
    T, D = x2.shape
    tm = min(TM_ROUTER, T)
    return pl.pallas_call(
        _router_kernel,
        grid=(T // tm,),
        in_specs=[pl.BlockSpec((tm, D), lambda i: (i, 0)), _const_spec(g.shape), _const_spec(wr.shape),
                  _const_spec(br.shape)],
        out_specs=[pl.BlockSpec((tm, D), lambda i: (i, 0)),
                   pl.BlockSpec((8, tm), lambda i: (0, i)),
                   pl.BlockSpec((8, tm), lambda i: (0, i)),
                   pl.BlockSpec((MOE_EXPERTS, LANES), lambda i: (0, 0))],
        out_shape=[jax.ShapeDtypeStruct((T, D), F32),
                   jax.ShapeDtypeStruct((8, T), I32),
                   jax.ShapeDtypeStruct((8, T), F32),
                   jax.ShapeDtypeStruct((MOE_EXPERTS, LANES), I32)],
        scratch_shapes=[pltpu.VMEM((MOE_EXPERTS, LANES), F32)],
        compiler_params=pltpu.CompilerParams(dimension_semantics=("arbitrary",),
                                             vmem_limit_bytes=VMEM_LIMIT),
        name="moe_router",
    )(x2, g, wr, br)


def _row_copy(src_ref, src_row, dst_ref, dst_row, sem):
    return pltpu.make_async_copy(src_ref.at[pl.ds(src_row, 1)], dst_ref.at[pl.ds(dst_row, 1)], sem)


def _dispatch_kernel(dest_ref, hf_ref, xin_ref, xbuf_ref, sem):
    del xin_ref
    tm = hf_ref.shape[0]

    def issue(i, c):
        _row_copy(hf_ref, i, xbuf_ref, dest_ref[0, 0, i], sem).start()
        _row_copy(hf_ref, i, xbuf_ref, dest_ref[0, 0, tm + i], sem).start()
        return c

    lax.fori_loop(0, tm, issue, 0)

    def drain(i, c):
        _row_copy(hf_ref, 0, xbuf_ref, 0, sem).wait()
        _row_copy(hf_ref, 0, xbuf_ref, 0, sem).wait()
        return c

    lax.fori_loop(0, tm, drain, 0)


def _dispatch(hf, dest3, n_rows):
    T, D = hf.shape
    tm = dest3.shape[2] // 2
    xzero = jnp.zeros((n_rows, D), F32)
    return pl.pallas_call(
        _dispatch_kernel,
        grid=(T // tm,),
        in_specs=[pl.BlockSpec((1, 1, 2 * tm), lambda i: (i, 0, 0), memory_space=pltpu.SMEM),
                  pl.BlockSpec((tm, D), lambda i: (i, 0)),
                  pl.BlockSpec(memory_space=pl.ANY)],
        out_specs=pl.BlockSpec(memory_space=pl.ANY),
        out_shape=jax.ShapeDtypeStruct((n_rows, D), F32),
        scratch_shapes=[pltpu.SemaphoreType.DMA(())],
        input_output_aliases={2: 0},
        compiler_params=pltpu.CompilerParams(dimension_semantics=("arbitrary",),
                                             vmem_limit_bytes=VMEM_LIMIT),
        name="moe_dispatch",
    )(dest3, hf, xzero)


def _expert_kernel(blk_e_ref, nb_ref, x_ref, wg_ref, wu_ref, wd_ref, y_ref):
    del blk_e_ref

    used = pl.program_id(0) < nb_ref[0]

    @pl.when(used)
    def _():
        xb = x_ref[...].astype(BF16)
        a = _dot(xb, wg_ref[...].astype(BF16))
        u = _dot(xb, wu_ref[...].astype(BF16))
        y_ref[...] = _dot((_silu(a) * u).astype(BF16), wd_ref[...].astype(BF16))

    @pl.when(jnp.logical_not(used))
    def _():
        y_ref[...] = jnp.zeros_like(y_ref)


def _experts(xbuf, blk_e, nb_used, wg, wu, wd):
    P, D = xbuf.shape
    bm = BM_EXPERT
    FF = wg.shape[-1]

    def row_map(i, be, nb):
        return (i, 0)

    def w_map(i, be, nb):
        return (be[jnp.minimum(i, nb[0] - 1)], 0, 0)

    return pl.pallas_call(
        _expert_kernel,
        grid_spec=pltpu.PrefetchScalarGridSpec(
            num_scalar_prefetch=2,
            grid=(P // bm,),
            in_specs=[pl.BlockSpec((bm, D), row_map),
                      pl.BlockSpec((None, D, FF), w_map),
                      pl.BlockSpec((None, D, FF), w_map),
                      pl.BlockSpec((None, FF, D), w_map)],
            out_specs=pl.BlockSpec((bm, D), row_map)),
        out_shape=jax.ShapeDtypeStruct((P, D), F32),
        compiler_params=pltpu.CompilerParams(dimension_semantics=("arbitrary",),
                                             vmem_limit_bytes=VMEM_LIMIT),
        name="moe_experts",
    )(blk_e, nb_used, xbuf, wg, wu, wd)


def _combine_kernel(dest_ref, x_ref, w_ref, gfin_ref, ybuf_ref, o_ref, r0, r1, sem, *, final_norm):
    tm = x_ref.shape[0]

    def issue(i, c):
        _row_copy(ybuf_ref, dest_ref[0, 0, i], r0, i, sem).start()
        _row_copy(ybuf_ref, dest_ref[0, 0, tm + i], r1, i, sem).start()
        return c

    lax.fori_loop(0, tm, issue, 0)

    def drain(i, c):
        _row_copy(ybuf_ref, 0, r0, 0, sem).wait()
        _row_copy(ybuf_ref, 0, r1, 0, sem).wait()
        return c

    lax.fori_loop(0, tm, drain, 0)
    w = w_ref[...]
    out = x_ref[...] + (r0[...] * w[:, 0:1] + r1[...] * w[:, 1:2])
    if final_norm:
        out = _rmsnorm(out, gfin_ref[...])
    o_ref[...] = out


def _combine(x2, ybuf, dest3, wt, gfin, final_norm):
    T, D = x2.shape
    tm = dest3.shape[2] // 2
    return pl.pallas_call(
        functools.partial(_combine_kernel, final_norm=final_norm),
        grid=(T // tm,),
        in_specs=[pl.BlockSpec((1, 1, 2 * tm), lambda i: (i, 0, 0), memory_space=pltpu.SMEM),
                  pl.BlockSpec((tm, D), lambda i: (i, 0)),
                  pl.BlockSpec((tm, MOE_TOPK), lambda i: (i, 0)),
                  _const_spec(gfin.shape),
                  pl.BlockSpec(memory_space=pl.ANY)],
        out_specs=pl.BlockSpec((tm, D), lambda i: (i, 0)),
        out_shape=jax.ShapeDtypeStruct((T, D), F32),
        scratch_shapes=[pltpu.VMEM((tm, D), F32), pltpu.VMEM((tm, D), F32), pltpu.SemaphoreType.DMA(())],
        compiler_params=pltpu.CompilerParams(dimension_semantics=("arbitrary",),
                                             vmem_limit_bytes=VMEM_LIMIT),
        name="moe_combine",
    )(dest3, x2, wt, gfin, ybuf)


def _moe(x2, g_ffn, wr, br, wg, wu, wd, gfin, final_norm):
    T, D = x2.shape
    E = MOE_EXPERTS
    bm = BM_EXPERT
    A = T * MOE_TOPK
    n_blocks = -(-A // bm) + E
    hf, mi, mf, cnt = _router(x2, g_ffn, wr, br)
    counts = cnt[:, 0]
    pcounts = (counts + bm - 1) // bm * bm
    pends = jnp.cumsum(pcounts)
    pstarts = pends - pcounts
    blk_e = jnp.minimum(jnp.searchsorted(pends, jnp.arange(n_blocks, dtype=I32) * bm, side='right'),
                        E - 1).astype(I32)
    nb_used = jnp.maximum(pends[-1:] // bm, 1).astype(I32)
    dest = pstarts[mi[0:2]].astype(I32) + mi[2:4]
    tm = min(TM_MOVE, T)
    dest3 = dest.reshape(2, T // tm, tm).transpose(1, 0, 2).reshape(T // tm, 1, 2 * tm)
    xbuf = _dispatch(hf, dest3, n_blocks * bm)
    ybuf = _experts(xbuf, blk_e, nb_used, wg, wu, wd)
    return _combine(x2, ybuf, dest3, mf[0:2].T, gfin, final_norm)


def kernel(x, positions, g_mix, w_in, ret_norm, w_ret_out, gla_w_a2, gla_b_a, gla_norm, w_gla_out, ssd_conv_w, ssd_conv_b, ssd_dt_bias, ssd_a_log, ssd_d, ssd_norm, w_ssd_out, w_o, g_ffn, w_rg, b_rg, w_re, b_re, w_exp_gate, w_exp_up, w_exp_down, g_final):
    B, S, D = x.shape
    L = w_in.shape[0]
    T = B * S

    def cols(a, b):
        return w_in[:, :, a:b]

    def padc(w, n):
        return jnp.pad(w, ((0, 0), (0, 0), (0, n - w.shape[-1])))

    w_a = jnp.concatenate([cols(O_RQ, O_GQ), cols(O_GTA, O_GTB)], axis=-1).astype(BF16)
    w_b = jnp.concatenate([cols(O_GQ, O_GA), cols(O_GTB, O_GTC), padc(cols(O_GA, O_SZ), LANES)],
                          axis=-1).astype(BF16)
    w_c = jnp.concatenate([cols(O_SZ, O_SDT), cols(O_GTC, O_GTC + D_MODEL), padc(cols(O_SDT, O_GTA), LANES)],
                          axis=-1).astype(BF16)
    wa2 = jnp.pad(gla_w_a2, ((0, 0), (0, LANES - GLA_RANK), (0, 0))).astype(BF16)
    padl = lambda v: jnp.pad(v, ((0, 0), (0, LANES - v.shape[-1])))[:, None, :]
    dtb = padl(ssd_dt_bias.astype(F32))
    a_row = padl(-jnp.exp(ssd_a_log.astype(F32)))
    dx = jnp.repeat(ssd_d.astype(F32), SSD_HEADDIM, axis=-1)[:, None, :]
    half = RET_DK // 2
    inv = ROPE_BASE ** (-jnp.arange(half, dtype=F32) / half)
    inv2 = jnp.concatenate([inv, inv])[None, :]
    posf = positions.astype(F32)[:, :, None]
    wr = jnp.concatenate([jnp.pad(jnp.swapaxes(w_rg, 1, 2), ((0, 0), (0, 8 - MOE_GROUPS), (0, 0))),
                          jnp.swapaxes(w_re, 1, 2)], axis=1).astype(F32)
    br = jnp.concatenate([jnp.pad(b_rg, ((0, 0), (0, 8 - MOE_GROUPS))), b_re], axis=1).astype(F32)[:, :, None]
    row = lambda v: v[:, None, :].astype(F32)
    g_mix_r, g_ffn_r = row(g_mix), row(g_ffn)
    ret_norm_r, gla_norm_r, ssd_norm_r = row(ret_norm), row(gla_norm), row(ssd_norm)
    gla_ba_r, conv_b_r = row(gla_b_a), row(ssd_conv_b)
    w_ret_o, w_gla_o = w_ret_out.astype(BF16), w_gla_out.astype(BF16)
    w_ssd_o, w_o_b = w_ssd_out.astype(BF16), w_o.astype(BF16)
    gfin = g_final[None, :].astype(F32)

    for l in range(L):
        m = _mixer_a(x, posf, inv2, g_mix_r[l], w_a[l], ret_norm_r[l], w_ret_o[l])
        m = _mixer_b(x, m, g_mix_r[l], w_b[l], wa2[l], gla_ba_r[l], gla_norm_r[l], w_gla_o[l])
        x = _mixer_c(x, m, g_mix_r[l], w_c[l], ssd_conv_w[l].astype(F32), conv_b_r[l], dtb[l], a_row[l],
                     dx[l], ssd_norm_r[l], w_ssd_o[l], w_o_b[l])
        x = _moe(x.reshape(T, D), g_ffn_r[l], wr[l], br[l], w_exp_gate[l], w_exp_up[l], w_exp_down[l],
                 gfin, l == L - 1).reshape(B, S, D)
    return x
```

```python
import functools
import math

import jax
import jax.numpy as jnp
from jax import lax
from jax.experimental import pallas as pl
from jax.experimental.pallas import tpu as pltpu

F32 = jnp.float32
BF16 = jnp.bfloat16
I32 = jnp.int32

D_MODEL = 1024
RET_HEADS, RET_DK, RET_DV, RET_CHUNK = 4, 128, 256, 128
GLA_HEADS, GLA_DK, GLA_DV, GLA_RANK, GLA_TAU, GLA_CHUNK = 4, 128, 256, 16, 16.0, 64
SSD_INNER, SSD_HEADDIM, SSD_GROUPS, SSD_STATE, SSD_CONV, SSD_CHUNK = 2048, 64, 4, 128, 4, 128
SSD_HEADS = SSD_INNER // SSD_HEADDIM
SSD_HEADS_PER_GROUP = SSD_HEADS // SSD_GROUPS
SSD_CONV_DIM = SSD_INNER + 2 * SSD_GROUPS * SSD_STATE
MOE_GROUPS, MOE_PER_GROUP, MOE_TOPK, MOE_FF = 4, 8, 2, 512
MOE_EXPERTS = MOE_GROUPS * MOE_PER_GROUP
ROPE_BASE = 10000.0
EPS = 1e-6
LANES = 128

RET_QK = RET_HEADS * RET_DK
RET_VW = RET_HEADS * RET_DV
GLA_QK = GLA_HEADS * GLA_DK
GLA_VW = GLA_HEADS * GLA_DV
SPLIT_SIZES = (RET_QK, RET_QK, RET_VW, RET_VW,
               GLA_QK, GLA_QK, GLA_VW, GLA_VW, GLA_RANK,
               SSD_INNER, SSD_CONV_DIM, SSD_HEADS,
               D_MODEL, D_MODEL, D_MODEL)
_OFFS = [0]
for _s in SPLIT_SIZES:
    _OFFS.append(_OFFS[-1] + _s)
(O_RQ, O_RK, O_RV, O_RG, O_GQ, O_GK, O_GV, O_GR, O_GA, O_SZ, O_SXBC, O_SDT, O_GTA, O_GTB, O_GTC, _) = _OFFS

TS_A = 512
TS_B = 512
TS_C = 256
TM_ROUTER = 512
TM_MOVE = 256
MOVE_UNROLL = 8
BM_EXPERT = 512
CONV_PAD = 8
VMEM_LIMIT = 56 * 1024 * 1024


def _dot(a, b):
    return jnp.dot(a, b, preferred_element_type=F32)


def _dot_nt(a, b):
    return lax.dot_general(a, b, (((1,), (1,)), ((), ())), preferred_element_type=F32)


def _dot_tn(a, b):
    return lax.dot_general(a, b, (((0,), (0,)), ((), ())), preferred_element_type=F32)


def _dot_hi(a, b):
    return jnp.dot(a, b, preferred_element_type=F32, precision=lax.Precision.HIGHEST)


def _dot_tn_hi(a, b):
    return lax.dot_general(a, b, (((0,), (0,)), ((), ())), preferred_element_type=F32,
                           precision=lax.Precision.HIGHEST)


def _dot_nt_hi(a, b):
    return lax.dot_general(a, b, (((1,), (1,)), ((), ())), preferred_element_type=F32,
                           precision=lax.Precision.HIGHEST)


def _rmsnorm(x, g):
    return x * lax.rsqrt(jnp.mean(x * x, axis=-1, keepdims=True) + EPS) * g


def _silu(x):
    return x * jax.nn.sigmoid(x)


def _const_spec(shape):
    nd = len(shape)
    return pl.BlockSpec(shape, lambda *_: (0,) * nd, pipeline_mode=pl.Buffered(1))


def _seq_spec(ts, width):
    return pl.BlockSpec((None, ts, width), lambda b, s: (b, s, 0))


def _rope_kernel(pos_ref, inv_ref, cos_ref, sin_ref):
    ang = pos_ref[...] * inv_ref[...]
    lane = lax.broadcasted_iota(I32, (1, LANES), 1)
    cos_ref[...] = jnp.cos(ang)
    sin_ref[...] = jnp.sin(ang) * jnp.where(lane < RET_DK // 2, -1.0, 1.0)


def _rope_tables(posf, inv2):
    B, S, _ = posf.shape
    ts = min(2048, S)
    return pl.pallas_call(
        _rope_kernel,
        grid=(B, S // ts),
        in_specs=[_seq_spec(ts, 1), _const_spec(inv2.shape)],
        out_specs=[_seq_spec(ts, LANES), _seq_spec(ts, LANES)],
        out_shape=[jax.ShapeDtypeStruct((B, S, LANES), F32)] * 2,
        compiler_params=pltpu.CompilerParams(dimension_semantics=("arbitrary", "arbitrary")),
        name="rope_tables",
    )(posf, inv2)


def _mixer_a_kernel(x_ref, cos_ref, sin_ref, g_ref, w_ref, nrm_ref, wout_ref, o_ref,
                    h_scr, o_scr, st_ref):
    C = RET_CHUNK
    ts = x_ref.shape[0]

    @pl.when(pl.program_id(1) == 0)
    def _():
        st_ref[...] = jnp.zeros_like(st_ref)

    h_scr[...] = _rmsnorm(x_ref[...], g_ref[...]).astype(BF16)
    h = h_scr[...]

    cos2 = cos_ref[...]
    sin2 = sin_ref[...]

    def rot(t):
        return t * cos2 + pltpu.roll(t, RET_DK // 2, 1) * sin2

    ii = lax.broadcasted_iota(I32, (C, C), 0)
    jj = lax.broadcasted_iota(I32, (C, C), 1)
    dif = (ii - jj).astype(F32)
    row = lax.broadcasted_iota(I32, (C, 1), 0).astype(F32)

    for hd in range(RET_HEADS):
        lg = math.log1p(-(2.0 ** (-5.0 - hd)))
        dmat = jnp.where(dif >= 0, jnp.exp(jnp.maximum(dif, 0.0) * lg), 0.0)
        qdec = jnp.exp((row + 1.0) * lg)
        kdec = jnp.exp((C - 1.0 - row) * lg)
        cdec = math.exp(C * lg)
        q = rot(_dot(h, w_ref[:, hd * RET_DK:(hd + 1) * RET_DK]))
        k = rot(_dot(h, w_ref[:, RET_QK + hd * RET_DK:RET_QK + (hd + 1) * RET_DK])) * (RET_DK ** -0.5)
        v = _dot(h, w_ref[:, 2 * RET_QK + hd * RET_DV:2 * RET_QK + (hd + 1) * RET_DV])
        g = _dot(h, w_ref[:, 2 * RET_QK + RET_VW + hd * RET_DV:2 * RET_QK + RET_VW + (hd + 1) * RET_DV])
        nrm = nrm_ref[:, hd * RET_DV:(hd + 1) * RET_DV]
        for c in range(ts // C):
            sl = slice(c * C, (c + 1) * C)
            qc, kc = q[sl], k[sl]
            vcb = v[sl].astype(BF16)
            sc = _dot_nt(qc.astype(BF16), kc.astype(BF16)) * dmat
            st = st_ref[hd]
            o = _dot(sc.astype(BF16), vcb) + _dot((qc * qdec).astype(BF16), st.astype(BF16))
            st_ref[hd] = cdec * st + _dot_tn((kc * kdec).astype(BF16), vcb)
            cen = o - jnp.mean(o, axis=-1, keepdims=True)
            y = cen * lax.rsqrt(jnp.mean(cen * cen, axis=-1, keepdims=True) + EPS) * nrm
            o_scr[sl, hd * RET_DV:(hd + 1) * RET_DV] = (y * _silu(g[sl])).astype(BF16)

    ya = _dot(o_scr[...], wout_ref[...])
    gate = jax.nn.sigmoid(_dot(h, w_ref[:, 2 * RET_QK + 2 * RET_VW:]))
    o_ref[...] = ya * gate


def _mixer_a(x, cos2, sin2, g, w, nrm, wout):
    B, S, D = x.shape
    ts = min(TS_A, S)
    return pl.pallas_call(
        _mixer_a_kernel,
        grid=(B, S // ts),
        in_specs=[_seq_spec(ts, D), _seq_spec(ts, LANES), _seq_spec(ts, LANES), _const_spec(g.shape),
                  _const_spec(w.shape), _const_spec(nrm.shape), _const_spec(wout.shape)],
        out_specs=_seq_spec(ts, D),
        out_shape=jax.ShapeDtypeStruct((B, S, D), F32),
        scratch_shapes=[pltpu.VMEM((ts, D), BF16), pltpu.VMEM((ts, RET_VW), BF16),
                        pltpu.VMEM((RET_HEADS, RET_DK, RET_DV), F32)],
        compiler_params=pltpu.CompilerParams(dimension_semantics=("arbitrary", "arbitrary"),
                                             vmem_limit_bytes=VMEM_LIMIT),
        name="mixer_a",
    )(x, cos2, sin2, g, w, nrm, wout)


def _mixer_b_kernel(x_ref, m_ref, g_ref, w_ref, wa2_ref, ba_ref, nrm_ref, wout_ref, o_ref,
                    h_scr, qi_scr, ki_scr, qe_scr, ks_scr, dec_scr, og_scr, o_scr, st_ref):
    C = GLA_CHUNK
    ts = x_ref.shape[0]

    @pl.when(pl.program_id(1) == 0)
    def _():
        st_ref[...] = jnp.zeros_like(st_ref)

    h_scr[...] = _rmsnorm(x_ref[...], g_ref[...]).astype(BF16)
    h = h_scr[...]

    o_ga = 2 * GLA_QK + 2 * GLA_VW + D_MODEL
    ga = _dot(h, w_ref[:, o_ga:o_ga + LANES])
    z = _dot(ga.astype(BF16), wa2_ref[...]) + ba_ref[...]
    la = (jnp.minimum(z, 0.0) - jnp.log1p(jnp.exp(-jnp.abs(z)))) * (1.0 / GLA_TAU)

    ii = lax.broadcasted_iota(I32, (C, C), 0)
    jj = lax.broadcasted_iota(I32, (C, C), 1)
    causal = ii >= jj
    tril = causal.astype(F32)

    q_all = _dot(h, w_ref[:, 0:GLA_QK]) * (GLA_DK ** -0.5)
    k_all = _dot(h, w_ref[:, GLA_QK:2 * GLA_QK])

    for c in range(ts // C):
        sl = slice(c * C, (c + 1) * C)
        b = _dot_hi(tril, la[sl])
        b_mid = b[C // 2:C // 2 + 1]
        b_last = b[C - 1:C]
        qc, kc = q_all[sl], k_all[sl]
        qi_scr[sl, :] = (qc * jnp.exp(b - b_mid)).astype(BF16)
        ki_scr[sl, :] = (kc * jnp.exp(b_mid - b)).astype(BF16)
        qe_scr[sl, :] = (qc * jnp.exp(b)).astype(BF16)
        ks_scr[sl, :] = (kc * jnp.exp(b_last - b)).astype(BF16)
        dec_scr[c:c + 1, :] = jnp.exp(b_last)

    for hd in range(GLA_HEADS):
        vb = _dot(h, w_ref[:, 2 * GLA_QK + hd * GLA_DV:2 * GLA_QK + (hd + 1) * GLA_DV]).astype(BF16)
        hs = slice(hd * GLA_DK, (hd + 1) * GLA_DK)
        st = st_ref[hd]
        for c in range(ts // C):
            sl = slice(c * C, (c + 1) * C)
            sc = jnp.where(causal, _dot_nt(qi_scr[sl, hs], ki_scr[sl, hs]), 0.0).astype(BF16)
            og_scr[sl, :] = _dot(sc, vb[sl]) + _dot_nt(qe_scr[sl, hs], st.astype(BF16))
            st = dec_scr[c:c + 1, hs] * st + _dot_tn(vb[sl], ks_scr[sl, hs])
        st_ref[hd] = st
        o = og_scr[...]
        y = o * lax.rsqrt(jnp.mean(o * o, axis=-1, keepdims=True) + EPS) * nrm_ref[:, hd * GLA_DV:(hd + 1) * GLA_DV]
        gr = _dot(h, w_ref[:, 2 * GLA_QK + GLA_VW + hd * GLA_DV:2 * GLA_QK + GLA_VW + (hd + 1) * GLA_DV])
        o_scr[:, hd * GLA_DV:(hd + 1) * GLA_DV] = (y * _silu(gr)).astype(BF16)

    yb = _dot(o_scr[...], wout_ref[...])
    gate = jax.nn.sigmoid(_dot(h, w_ref[:, 2 * GLA_QK + 2 * GLA_VW:2 * GLA_QK + 2 * GLA_VW + D_MODEL]))
    o_ref[...] = m_ref[...] + yb * gate


def _mixer_b(x, m, g, w, wa2, ba, nrm, wout):
    B, S, D = x.shape
    ts = min(TS_B, S)
    return pl.pallas_call(
        _mixer_b_kernel,
        grid=(B, S // ts),
        in_specs=[_seq_spec(ts, D), _seq_spec(ts, D), _const_spec(g.shape), _const_spec(w.shape),
                  _const_spec(wa2.shape), _const_spec(ba.shape), _const_spec(nrm.shape),
                  _const_spec(wout.shape)],
        out_specs=_seq_spec(ts, D),
        out_shape=jax.ShapeDtypeStruct((B, S, D), F32),
        scratch_shapes=[pltpu.VMEM((ts, D), BF16)] + [pltpu.VMEM((ts, GLA_QK), BF16)] * 4 + [
                        pltpu.VMEM((max(ts // GLA_CHUNK, 8), GLA_QK), F32),
                        pltpu.VMEM((ts, GLA_DV), F32),
                        pltpu.VMEM((ts, GLA_VW), BF16),
                        pltpu.VMEM((GLA_HEADS, GLA_DV, GLA_DK), F32)],
        compiler_params=pltpu.CompilerParams(dimension_semantics=("arbitrary", "arbitrary"),
                                             vmem_limit_bytes=VMEM_LIMIT),
        name="mixer_b",
    )(x, m, g, w, wa2, ba, nrm, wout)


def _mixer_c_kernel(x_ref, m_ref, g_ref, w_ref, cw_ref, cb_ref, dtb_ref, a_ref, dx_ref, nrm_ref,
                    wout_ref, wo_ref, o_ref,
                    h_scr, cbuf, xs_scr, bc_scr, z_scr, dt_scr, y_scr, st_ref):
    C = SSD_CHUNK
    N = SSD_STATE
    P = SSD_HEADDIM
    ts = x_ref.shape[0]
    GN = SSD_GROUPS * N

    @pl.when(pl.program_id(1) == 0)
    def _():
        st_ref[...] = jnp.zeros_like(st_ref)
        cbuf[...] = jnp.zeros_like(cbuf)

    h_scr[...] = _rmsnorm(x_ref[...], g_ref[...]).astype(BF16)
    h = h_scr[...]

    z_scr[...] = _silu(_dot(h, w_ref[:, 0:SSD_INNER]))
    o_gt = SSD_INNER + SSD_CONV_DIM
    o_dt = o_gt + D_MODEL
    dt_scr[...] = jax.nn.softplus(_dot(h, w_ref[:, o_dt:o_dt + LANES]) + dtb_ref[...])

    CB = 512
    for cblk in range(SSD_CONV_DIM // CB):
        cs = slice(cblk * CB, (cblk + 1) * CB)
        proj = _dot(h, w_ref[:, SSD_INNER + cblk * CB:SSD_INNER + (cblk + 1) * CB])
        ext = jnp.concatenate([cbuf[:, cs], proj], axis=0)
        acc = cb_ref[:, cs] + proj * cw_ref[SSD_CONV - 1:SSD_CONV, cs]
        for k in range(1, SSD_CONV):
            j = SSD_CONV - 1 - k
            acc = acc + pltpu.roll(ext, k, 0)[CONV_PAD:] * cw_ref[j:j + 1, cs]
        act = _silu(acc)
        if cblk * CB < SSD_INNER:
            xs_scr[:, cs] = act
        else:
            bc_scr[:, cblk * CB - SSD_INNER:(cblk + 1) * CB - SSD_INNER] = act.astype(BF16)
        cbuf[:, cs] = proj[ts - CONV_PAD:]

    ii = lax.broadcasted_iota(I32, (C, C), 0)
    jj = lax.broadcasted_iota(I32, (C, C), 1)
    causal = ii >= jj
    tril = causal.astype(F32)
    triu = (ii <= jj).astype(F32)
    a_row = a_ref[...]

    def chunk_body(c, carry):
        r0 = pl.multiple_of(c * C, C)
        rs = pl.ds(r0, C)
        dt = dt_scr[rs, :]
        dta = dt * a_row
        acs = _dot_hi(tril, dta)
        acs_t = _dot_tn_hi(dta, triu)
        dt_t = dt.T
        last_row = acs[C - 1:C, :]
        w1_t = dt_t * jnp.exp(acs_t[:, C - 1:C] - acs_t)
        for g in range(SSD_GROUPS):
            bcg = bc_scr[rs, g * N:(g + 1) * N]
            ccg = bc_scr[rs, GN + g * N:GN + (g + 1) * N]
            cb = _dot_nt(ccg, bcg)
            ccf = ccg.astype(F32)
            bct = bcg.astype(F32).T
            for r in range(SSD_HEADS_PER_GROUP):
                hh = g * SSD_HEADS_PER_GROUP + r
                col = jnp.broadcast_to(acs[:, hh:hh + 1], (C, C))
                seg = col - acs_t[hh:hh + 1, :]
                lmat = jnp.where(causal, jnp.exp(jnp.where(causal, seg, 0.0)), 0.0)
                a1 = (lmat * cb * dt_t[hh:hh + 1, :]).astype(BF16)
                a2 = (jnp.exp(col) * ccf).astype(BF16)
                xh = xs_scr[rs, hh * P:(hh + 1) * P]
                xhb = xh.astype(BF16)
                st = st_ref[hh]
                y = _dot(a1, xhb) + _dot(a2, st.astype(BF16))
                new = _dot((bct * w1_t[hh:hh + 1, :]).astype(BF16), xhb)
                st_ref[hh] = jnp.exp(last_row[:, hh:hh + 1]) * st + new
                y_scr[rs, hh * P:(hh + 1) * P] = y
        return carry

    lax.fori_loop(0, ts // C, chunk_body, 0)

    GW = SSD_INNER // SSD_GROUPS
    for g in range(SSD_GROUPS):
        gs = slice(g * GW, (g + 1) * GW)
        y = (y_scr[:, gs] + xs_scr[:, gs] * dx_ref[:, gs]) * z_scr[:, gs]
        y = y * lax.rsqrt(jnp.mean(y * y, axis=-1, keepdims=True) + EPS) * nrm_ref[:, gs]
        y_scr[:, gs] = y
    yc = _dot(y_scr[...].astype(BF16), wout_ref[...])
    gate = jax.nn.sigmoid(_dot(h, w_ref[:, o_gt:o_gt + D_MODEL]))
    merged = m_ref[...] + yc * gate
    o_ref[...] = x_ref[...] + _dot(merged.astype(BF16), wo_ref[...])


def _mixer_c(x, m, g, w, cw, cb, dtb, a_row, dx, nrm, wout, wo):
    B, S, D = x.shape
    ts = min(TS_C, S)
    consts = (g, w, cw, cb, dtb, a_row, dx, nrm, wout, wo)
    return pl.pallas_call(
        _mixer_c_kernel,
        grid=(B, S // ts),
        in_specs=[_seq_spec(ts, D), _seq_spec(ts, D)] + [_const_spec(c.shape) for c in consts],
        out_specs=_seq_spec(ts, D),
        out_shape=jax.ShapeDtypeStruct((B, S, D), F32),
        scratch_shapes=[pltpu.VMEM((ts, D), BF16),
                        pltpu.VMEM((CONV_PAD, SSD_CONV_DIM), F32),
                        pltpu.VMEM((ts, SSD_INNER), F32),
                        pltpu.VMEM((ts, 2 * SSD_GROUPS * SSD_STATE), BF16),
                        pltpu.VMEM((ts, SSD_INNER), F32),
                        pltpu.VMEM((ts, LANES), F32),
                        pltpu.VMEM((ts, SSD_INNER), F32),
                        pltpu.VMEM((SSD_HEADS, SSD_STATE, SSD_HEADDIM), F32)],
        compiler_params=pltpu.CompilerParams(dimension_semantics=("arbitrary", "arbitrary"),
                                             vmem_limit_bytes=VMEM_LIMIT),
        name="mixer_c",
    )(x, m, *consts)


ROUTER_ROWS = 8 + MOE_EXPERTS


def _router_kernel(x_ref, g_ref, wr_ref, br_ref, hf_ref, mi_ref, mf_ref, cnt_ref, cnt_scr):
    tm = x_ref.shape[0]
    E = MOE_EXPERTS
    NEG = -jnp.inf

    @pl.when(pl.program_id(0) == 0)
    def _():
        cnt_scr[...] = jnp.zeros_like(cnt_scr)

    hf = _rmsnorm(x_ref[...], g_ref[...])
    hf_ref[...] = hf
    logits = _dot_nt_hi(wr_ref[...], hf) + br_ref[...]

    sub8 = lax.broadcasted_iota(I32, (8, tm), 0)
    gl = jnp.where(sub8 < MOE_GROUPS, logits[0:8], NEG)
    gmax = jnp.max(gl, axis=0, keepdims=True)
    g_sel = jnp.min(jnp.where(gl == gmax, sub8, 8), axis=0, keepdims=True)
    p_sel = 1.0 / jnp.sum(jnp.exp(gl - gmax), axis=0, keepdims=True)

    el = jnp.zeros((MOE_PER_GROUP, tm), F32)
    for g in range(MOE_GROUPS):
        el = jnp.where(g_sel == g, logits[8 + g * MOE_PER_GROUP:8 + (g + 1) * MOE_PER_GROUP], el)
    m1 = jnp.max(el, axis=0, keepdims=True)
    i1 = jnp.min(jnp.where(el == m1, sub8, 8), axis=0, keepdims=True)
    el2 = jnp.where(sub8 == i1, NEG, el)
    m2 = jnp.max(el2, axis=0, keepdims=True)
    i2 = jnp.min(jnp.where(el2 == m2, sub8, 8), axis=0, keepdims=True)
    e2 = jnp.exp(m2 - m1)
    w0 = p_sel / (1.0 + e2)
    w1 = p_sel * e2 / (1.0 + e2)
    eid0 = g_sel * MOE_PER_GROUP + i1
    eid1 = g_sel * MOE_PER_GROUP + i2

    subE = lax.broadcasted_iota(I32, (E, tm), 0)
    oh0 = subE == eid0
    oh1 = subE == eid1
    oh = (oh0 | oh1).astype(BF16)
    t_r = lax.broadcasted_iota(I32, (tm, tm), 0)
    t_c = lax.broadcasted_iota(I32, (tm, tm), 1)
    before = (t_r < t_c).astype(BF16)
    tot = _dot(oh, before) + cnt_scr[:, 0:1]
    rank0 = jnp.sum(jnp.where(oh0, tot, 0.0), axis=0, keepdims=True)
    rank1 = jnp.sum(jnp.where(oh1, tot, 0.0), axis=0, keepdims=True)
    cnt_new = cnt_scr[...] + jnp.sum(oh.astype(F32), axis=1, keepdims=True)
    cnt_scr[...] = cnt_new
    cnt_ref[...] = cnt_new.astype(I32)

    mi_ref[...] = jnp.zeros_like(mi_ref)
    mi_ref[0:1, :] = eid0
    mi_ref[1:2, :] = eid1
    mi_ref[2:3, :] = rank0.astype(I32)
    mi_ref[3:4, :] = rank1.astype(I32)
    mf_ref[...] = jnp.zeros_like(mf_ref)
    mf_ref[0:1, :] = w0
    mf_ref[1:2, :] = w1


def _router(x2, g, wr, br):
    T, D = x2.shape
    tm = min(TM_ROUTER, T)
    return pl.pallas_call(
        _router_kernel,
        grid=(T // tm,),
        in_specs=[pl.BlockSpec((tm, D), lambda i: (i, 0)), _const_spec(g.shape), _const_spec(wr.shape),
                  _const_spec(br.shape)],
        out_specs=[pl.BlockSpec((tm, D), lambda i: (i, 0)),
                   pl.BlockSpec((8, tm), lambda i: (0, i)),
                   pl.BlockSpec((8, tm), lambda i: (0, i)),
                   pl.BlockSpec((MOE_EXPERTS, LANES), lambda i: (0, 0))],
        out_shape=[jax.ShapeDtypeStruct((T, D), F32),
                   jax.ShapeDtypeStruct((8, T), I32),
                   jax.ShapeDtypeStruct((8, T), F32),
                   jax.ShapeDtypeStruct((MOE_EXPERTS, LANES), I32)],
        scratch_shapes=[pltpu.VMEM((MOE_EXPERTS, LANES), F32)],
        compiler_params=pltpu.CompilerParams(dimension_semantics=("arbitrary",),
                                             vmem_limit_bytes=VMEM_LIMIT),
        name="moe_router",
    )(x2, g, wr, br)


def _row_copy(src_ref, src_row, dst_ref, dst_row, sem):
    return pltpu.make_async_copy(src_ref.at[pl.ds(src_row, 1)], dst_ref.at[pl.ds(dst_row, 1)], sem)


def _dest_kernel(ps_ref, mi_ref, o_ref):
    nt, _, tm2 = o_ref.shape
    tm = tm2 // 2
    eid = mi_ref[0:2, :]
    d = mi_ref[2:4, :]
    for e in range(MOE_EXPERTS):
        d = d + jnp.where(eid == e, ps_ref[e], 0)
    for j in range(nt):
        o_ref[j, :, 0:tm] = d[0:1, j * tm:(j + 1) * tm]
        o_ref[j, :, tm:tm2] = d[1:2, j * tm:(j + 1) * tm]


def _dest_slots(pstarts, mi, tm):
    T = mi.shape[1]
    nt = min(8, T // tm)
    return pl.pallas_call(
        _dest_kernel,
        grid_spec=pltpu.PrefetchScalarGridSpec(
            num_scalar_prefetch=1,
            grid=(T // (nt * tm),),
            in_specs=[pl.BlockSpec((8, nt * tm), lambda i, ps: (0, i))],
            out_specs=pl.BlockSpec((nt, 1, 2 * tm), lambda i, ps: (i, 0, 0))),
        out_shape=jax.ShapeDtypeStruct((T // tm, 1, 2 * tm), I32),
        compiler_params=pltpu.CompilerParams(dimension_semantics=("arbitrary",)),
        name="moe_dest",
    )(pstarts, mi)


def _dispatch_kernel(dest_ref, hf_ref, xin_ref, xbuf_ref, sem):
    del xin_ref
    i = pl.program_id(0)
    n = pl.num_programs(0)
    tm = dest_ref.shape[2] // 2
    base = i * tm

    def issue(j, c):
        for u in range(MOVE_UNROLL):
            r = j * MOVE_UNROLL + u
            _row_copy(hf_ref, base + r, xbuf_ref, dest_ref[0, 0, r], sem).start(priority=0)
            _row_copy(hf_ref, base + r, xbuf_ref, dest_ref[0, 0, tm + r], sem).start(priority=1)
        return c

    lax.fori_loop(0, tm // MOVE_UNROLL, issue, 0)

    def drain_tile():
        for _ in range(MOE_TOPK):
            pltpu.make_async_copy(hf_ref.at[pl.ds(0, tm)], xbuf_ref.at[pl.ds(0, tm)], sem).wait()

    @pl.when(i > 0)
    def _():
        drain_tile()

    @pl.when(i == n - 1)
    def _():
        drain_tile()


def _dispatch(hf, dest3, n_rows):
    T, D = hf.shape
    tm = dest3.shape[2] // 2
    xzero = jnp.zeros((n_rows, D), F32)
    return pl.pallas_call(
        _dispatch_kernel,
        grid=(T // tm,),
        in_specs=[pl.BlockSpec((1, 1, 2 * tm), lambda i: (i, 0, 0), memory_space=pltpu.SMEM),
                  pl.BlockSpec(memory_space=pl.ANY),
                  pl.BlockSpec(memory_space=pl.ANY)],
        out_specs=pl.BlockSpec(memory_space=pl.ANY),
        out_shape=jax.ShapeDtypeStruct((n_rows, D), F32),
        scratch_shapes=[pltpu.SemaphoreType.DMA(())],
        input_output_aliases={2: 0},
        compiler_params=pltpu.CompilerParams(dimension_semantics=("arbitrary",),
                                             vmem_limit_bytes=VMEM_LIMIT),
        name="moe_dispatch",
    )(dest3, hf, xzero)


def _expert_kernel(blk_e_ref, nb_ref, x_ref, wg_ref, wu_ref, wd_ref, y_ref):
    del blk_e_ref

    used = pl.program_id(0) < nb_ref[0]

    @pl.when(used)
    def _():
        xb = x_ref[...].astype(BF16)
        a = _dot(xb, wg_ref[...].astype(BF16))
        u = _dot(xb, wu_ref[...].astype(BF16))
        y_ref[...] = _dot((_silu(a) * u).astype(BF16), wd_ref[...].astype(BF16))

    @pl.when(jnp.logical_not(used))
    def _():
        y_ref[...] = jnp.zeros_like(y_ref)


def _experts(xbuf, blk_e, nb_used, wg, wu, wd):
    P, D = xbuf.shape
    bm = BM_EXPERT
    FF = wg.shape[-1]

    def row_map(i, be, nb):
        return (i, 0)

    def w_map(i, be, nb):
        return (be[jnp.minimum(i, nb[0] - 1)], 0, 0)

    return pl.pallas_call(
        _expert_kernel,
        grid_spec=pltpu.PrefetchScalarGridSpec(
            num_scalar_prefetch=2,
            grid=(P // bm,),
            in_specs=[pl.BlockSpec((bm, D), row_map),
                      pl.BlockSpec((None, D, FF), w_map),
                      pl.BlockSpec((None, D, FF), w_map),
                      pl.BlockSpec((None, FF, D), w_map)],
            out_specs=pl.BlockSpec((bm, D), row_map)),
        out_shape=jax.ShapeDtypeStruct((P, D), F32),
        compiler_params=pltpu.CompilerParams(dimension_semantics=("arbitrary",),
                                             vmem_limit_bytes=VMEM_LIMIT),
        name="moe_experts",
    )(blk_e, nb_used, xbuf, wg, wu, wd)


def _combine_kernel(dcur_ref, dnext_ref, x_ref, w_ref, gfin_ref, ybuf_ref, o_ref, rbuf, sem, *, final_norm):
    i = pl.program_id(0)
    n = pl.num_programs(0)
    tm = x_ref.shape[0]
    slot = i % 2

    def issue(dref, s):
        def body(j, c):
            for u in range(MOVE_UNROLL):
                r = j * MOVE_UNROLL + u
                _row_copy(ybuf_ref, dref[0, 0, r], rbuf.at[s, 0], r, sem.at[s]).start(priority=0)
                _row_copy(ybuf_ref, dref[0, 0, tm + r], rbuf.at[s, 1], r, sem.at[s]).start(priority=1)
            return c

        lax.fori_loop(0, tm // MOVE_UNROLL, body, 0)

    @pl.when(i == 0)
    def _():
        issue(dcur_ref, 0)

    @pl.when(i + 1 < n)
    def _():
        issue(dnext_ref, 1 - slot)

    for k in range(MOE_TOPK):
        pltpu.make_async_copy(ybuf_ref.at[pl.ds(0, tm)], rbuf.at[slot, k], sem.at[slot]).wait()
    w = w_ref[...]
    out = x_ref[...] + (rbuf[slot, 0] * w[:, 0:1] + rbuf[slot, 1] * w[:, 1:2])
    if final_norm:
        out = _rmsnorm(out, gfin_ref[...])
    o_ref[...] = out


def _combine(x2, ybuf, dest3, wt, gfin, final_norm):
    T, D = x2.shape
    tm = dest3.shape[2] // 2
    n = T // tm
    return pl.pallas_call(
        functools.partial(_combine_kernel, final_norm=final_norm),
        grid=(n,),
        in_specs=[pl.BlockSpec((1, 1, 2 * tm), lambda i: (i, 0, 0), memory_space=pltpu.SMEM),
                  pl.BlockSpec((1, 1, 2 * tm), lambda i: (jnp.minimum(i + 1, n - 1), 0, 0),
                               memory_space=pltpu.SMEM),
                  pl.BlockSpec((tm, D), lambda i: (i, 0)),
                  pl.BlockSpec((tm, MOE_TOPK), lambda i: (i, 0)),
                  _const_spec(gfin.shape),
                  pl.BlockSpec(memory_space=pl.ANY)],
        out_specs=pl.BlockSpec((tm, D), lambda i: (i, 0)),
        out_shape=jax.ShapeDtypeStruct((T, D), F32),
        scratch_shapes=[pltpu.VMEM((2, MOE_TOPK, tm, D), F32), pltpu.SemaphoreType.DMA((2,))],
        compiler_params=pltpu.CompilerParams(dimension_semantics=("arbitrary",),
                                             vmem_limit_bytes=VMEM_LIMIT),
        name="moe_combine",
    )(dest3, dest3, x2, wt, gfin, ybuf)


def _moe(x2, g_ffn, wr, br, wg, wu, wd, gfin, final_norm):
    T, D = x2.shape
    E = MOE_EXPERTS
    bm = BM_EXPERT
    A = T * MOE_TOPK
    n_blocks = -(-A // bm) + E
    hf, mi, mf, cnt = _router(x2, g_ffn, wr, br)
    counts = cnt[:, 0]
    pcounts = (counts + bm - 1) // bm * bm
    pends = jnp.cumsum(pcounts)
    pstarts = pends - pcounts
    blk_start = jnp.arange(n_blocks, dtype=I32)[:, None] * bm
    blk_e = jnp.minimum(jnp.sum((pends[None, :] <= blk_start).astype(I32), axis=1), E - 1)
    nb_used = jnp.maximum(pends[-1:] // bm, 1).astype(I32)
    tm = min(TM_MOVE, T)
    dest3 = _dest_slots(pstarts.astype(I32), mi, tm)
    xbuf = _dispatch(hf, dest3, n_blocks * bm)
    ybuf = _experts(xbuf, blk_e, nb_used, wg, wu, wd)
    return _combine(x2, ybuf, dest3, mf[0:2].T, gfin, final_norm)


def kernel(x, positions, g_mix, w_in, ret_norm, w_ret_out, gla_w_a2, gla_b_a, gla_norm, w_gla_out, ssd_conv_w, ssd_conv_b, ssd_dt_bias, ssd_a_log, ssd_d, ssd_norm, w_ssd_out, w_o, g_ffn, w_rg, b_rg, w_re, b_re, w_exp_gate, w_exp_up, w_exp_down, g_final):
    B, S, D = x.shape
    L = w_in.shape[0]
    T = B * S

    def cols(a, b):
        return w_in[:, :, a:b]

    def padc(w, n):
        return jnp.pad(w, ((0, 0), (0, 0), (0, n - w.shape[-1])))

    w_a = jnp.concatenate([cols(O_RQ, O_GQ), cols(O_GTA, O_GTB)], axis=-1).astype(BF16)
    w_b = jnp.concatenate([cols(O_GQ, O_GA), cols(O_GTB, O_GTC), padc(cols(O_GA, O_SZ), LANES)],
                          axis=-1).astype(BF16)
    w_c = jnp.concatenate([cols(O_SZ, O_SDT), cols(O_GTC, O_GTC + D_MODEL), padc(cols(O_SDT, O_GTA), LANES)],
                          axis=-1).astype(BF16)
    wa2 = jnp.pad(gla_w_a2, ((0, 0), (0, LANES - GLA_RANK), (0, 0))).astype(BF16)
    padl = lambda v: jnp.pad(v, ((0, 0), (0, LANES - v.shape[-1])))[:, None, :]
    dtb = padl(ssd_dt_bias.astype(F32))
    a_row = padl(-jnp.exp(ssd_a_log.astype(F32)))
    dx = jnp.repeat(ssd_d.astype(F32), SSD_HEADDIM, axis=-1)[:, None, :]
    half = RET_DK // 2
    inv = ROPE_BASE ** (-jnp.arange(half, dtype=F32) / half)
    inv2 = jnp.concatenate([inv, inv])[None, :]
    posf = positions.astype(F32)[:, :, None]
    wr = jnp.concatenate([jnp.pad(jnp.swapaxes(w_rg, 1, 2), ((0, 0), (0, 8 - MOE_GROUPS), (0, 0))),
                          jnp.swapaxes(w_re, 1, 2)], axis=1).astype(F32)
    br = jnp.concatenate([jnp.pad(b_rg, ((0, 0), (0, 8 - MOE_GROUPS))), b_re], axis=1).astype(F32)[:, :, None]
    row = lambda v: v[:, None, :].astype(F32)
    g_mix_r, g_ffn_r = row(g_mix), row(g_ffn)
    ret_norm_r, gla_norm_r, ssd_norm_r = row(ret_norm), row(gla_norm), row(ssd_norm)
    gla_ba_r, conv_b_r = row(gla_b_a), row(ssd_conv_b)
    w_ret_o, w_gla_o = w_ret_out.astype(BF16), w_gla_out.astype(BF16)
    w_ssd_o, w_o_b = w_ssd_out.astype(BF16), w_o.astype(BF16)
    gfin = g_final[None, :].astype(F32)

    cos2, sin2 = _rope_tables(posf, inv2)
    for l in range(L):
        m = _mixer_a(x, cos2, sin2, g_mix_r[l], w_a[l], ret_norm_r[l], w_ret_o[l])
        m = _mixer_b(x, m, g_mix_r[l], w_b[l], wa2[l], gla_ba_r[l], gla_norm_r[l], w_gla_o[l])
        x = _mixer_c(x, m, g_mix_r[l], w_c[l], ssd_conv_w[l].astype(F32), conv_b_r[l], dtb[l], a_row[l],
                     dx[l], ssd_norm_r[l], w_ssd_o[l], w_o_b[l])
        x = _moe(x.reshape(T, D), g_ffn_r[l], wr[l], br[l], w_exp_gate[l], w_exp_up[l], w_exp_down[l],
                 gfin, l == L - 1).reshape(B, S, D)
    return x
```

```python
import functools
import math

import jax
import jax.numpy as jnp
from jax import lax
from jax.experimental import pallas as pl
from jax.experimental.pallas import tpu as pltpu

F32 = jnp.float32
BF16 = jnp.bfloat16
I32 = jnp.int32

D_MODEL = 1024
RET_HEADS, RET_DK, RET_DV, RET_CHUNK = 4, 128, 256, 128
GLA_HEADS, GLA_DK, GLA_DV, GLA_RANK, GLA_TAU, GLA_CHUNK = 4, 128, 256, 16, 16.0, 64
SSD_INNER, SSD_HEADDIM, SSD_GROUPS, SSD_STATE, SSD_CONV, SSD_CHUNK = 2048, 64, 4, 128, 4, 128
SSD_HEADS = SSD_INNER // SSD_HEADDIM
SSD_HEADS_PER_GROUP = SSD_HEADS // SSD_GROUPS
SSD_CONV_DIM = SSD_INNER + 2 * SSD_GROUPS * SSD_STATE
MOE_GROUPS, MOE_PER_GROUP, MOE_TOPK, MOE_FF = 4, 8, 2, 512
MOE_EXPERTS = MOE_GROUPS * MOE_PER_GROUP
ROPE_BASE = 10000.0
EPS = 1e-6
LANES = 128

RET_QK = RET_HEADS * RET_DK
RET_VW = RET_HEADS * RET_DV
GLA_QK = GLA_HEADS * GLA_DK
GLA_VW = GLA_HEADS * GLA_DV
SPLIT_SIZES = (RET_QK, RET_QK, RET_VW, RET_VW,
               GLA_QK, GLA_QK, GLA_VW, GLA_VW, GLA_RANK,
               SSD_INNER, SSD_CONV_DIM, SSD_HEADS,
               D_MODEL, D_MODEL, D_MODEL)
_OFFS = [0]
for _s in SPLIT_SIZES:
    _OFFS.append(_OFFS[-1] + _s)
(O_RQ, O_RK, O_RV, O_RG, O_GQ, O_GK, O_GV, O_GR, O_GA, O_SZ, O_SXBC, O_SDT, O_GTA, O_GTB, O_GTC, _) = _OFFS

TS_A = 512
TS_B = 512
TS_C = 256
TM_ROUTER = 512
TM_MOVE = 256
MOVE_UNROLL = 8
BM_EXPERT = 512
CONV_PAD = 8
VMEM_LIMIT = 56 * 1024 * 1024


def _dot(a, b):
    return jnp.dot(a, b, preferred_element_type=F32)


def _dot_nt(a, b):
    return lax.dot_general(a, b, (((1,), (1,)), ((), ())), preferred_element_type=F32)


def _dot_tn(a, b):
    return lax.dot_general(a, b, (((0,), (0,)), ((), ())), preferred_element_type=F32)


def _dot_hi(a, b):
    return jnp.dot(a, b, preferred_element_type=F32, precision=lax.Precision.HIGHEST)


def _dot_tn_hi(a, b):
    return lax.dot_general(a, b, (((0,), (0,)), ((), ())), preferred_element_type=F32,
                           precision=lax.Precision.HIGHEST)


def _dot_nt_hi(a, b):
    return lax.dot_general(a, b, (((1,), (1,)), ((), ())), preferred_element_type=F32,
                           precision=lax.Precision.HIGHEST)


def _rmsnorm(x, g):
    return x * lax.rsqrt(jnp.mean(x * x, axis=-1, keepdims=True) + EPS) * g


def _silu(x):
    return x * jax.nn.sigmoid(x)


def _const_spec(shape):
    nd = len(shape)
    return pl.BlockSpec(shape, lambda *_: (0,) * nd, pipeline_mode=pl.Buffered(1))


def _seq_spec(ts, width):
    return pl.BlockSpec((None, ts, width), lambda b, s: (b, s, 0))


def _rope_kernel(pos_ref, inv_ref, cos_ref, sin_ref):
    ang = pos_ref[...] * inv_ref[...]
    lane = lax.broadcasted_iota(I32, (1, LANES), 1)
    cos_ref[...] = jnp.cos(ang)
    sin_ref[...] = jnp.sin(ang) * jnp.where(lane < RET_DK // 2, -1.0, 1.0)


def _rope_tables(posf, inv2):
    B, S, _ = posf.shape
    ts = min(2048, S)
    return pl.pallas_call(
        _rope_kernel,
        grid=(B, S // ts),
        in_specs=[_seq_spec(ts, 1), _const_spec(inv2.shape)],
        out_specs=[_seq_spec(ts, LANES), _seq_spec(ts, LANES)],
        out_shape=[jax.ShapeDtypeStruct((B, S, LANES), F32)] * 2,
        compiler_params=pltpu.CompilerParams(dimension_semantics=("arbitrary", "arbitrary")),
        name="rope_tables",
    )(posf, inv2)


def _mixer_a_kernel(x_ref, cos_ref, sin_ref, g_ref, w_ref, nrm_ref, wout_ref, o_ref,
                    h_scr, o_scr, st_ref):
    C = RET_CHUNK
    ts = x_ref.shape[0]

    @pl.when(pl.program_id(1) == 0)
    def _():
        st_ref[...] = jnp.zeros_like(st_ref)

    h_scr[...] = _rmsnorm(x_ref[...], g_ref[...]).astype(BF16)
    h = h_scr[...]

    cos2 = cos_ref[...]
    sin2 = sin_ref[...]

    def rot(t):
        return t * cos2 + pltpu.roll(t, RET_DK // 2, 1) * sin2

    ii = lax.broadcasted_iota(I32, (C, C), 0)
    jj = lax.broadcasted_iota(I32, (C, C), 1)
    dif = (ii - jj).astype(F32)
    row = lax.broadcasted_iota(I32, (C, 1), 0).astype(F32)

    q_all = _dot(h, w_ref[:, 0:RET_QK])
    k_all = _dot(h, w_ref[:, RET_QK:2 * RET_QK])

    for hd in range(RET_HEADS):
        lg = math.log1p(-(2.0 ** (-5.0 - hd)))
        dmat = jnp.where(dif >= 0, jnp.exp(jnp.maximum(dif, 0.0) * lg), 0.0)
        qdec = jnp.exp((row + 1.0) * lg)
        kdec = jnp.exp((C - 1.0 - row) * lg)
        cdec = math.exp(C * lg)
        q = rot(q_all[:, hd * RET_DK:(hd + 1) * RET_DK])
        k = rot(k_all[:, hd * RET_DK:(hd + 1) * RET_DK]) * (RET_DK ** -0.5)
        v = _dot(h, w_ref[:, 2 * RET_QK + hd * RET_DV:2 * RET_QK + (hd + 1) * RET_DV])
        g = _dot(h, w_ref[:, 2 * RET_QK + RET_VW + hd * RET_DV:2 * RET_QK + RET_VW + (hd + 1) * RET_DV])
        nrm = nrm_ref[:, hd * RET_DV:(hd + 1) * RET_DV]
        for c in range(ts // C):
            sl = slice(c * C, (c + 1) * C)
            qc, kc = q[sl], k[sl]
            vcb = v[sl].astype(BF16)
            sc = _dot_nt(qc.astype(BF16), kc.astype(BF16)) * dmat
            st = st_ref[hd]
            o = _dot(sc.astype(BF16), vcb) + _dot((qc * qdec).astype(BF16), st.astype(BF16))
            st_ref[hd] = cdec * st + _dot_tn((kc * kdec).astype(BF16), vcb)
            cen = o - jnp.mean(o, axis=-1, keepdims=True)
            y = cen * lax.rsqrt(jnp.mean(cen * cen, axis=-1, keepdims=True) + EPS) * nrm
            o_scr[sl, hd * RET_DV:(hd + 1) * RET_DV] = (y * _silu(g[sl])).astype(BF16)

    ya = _dot(o_scr[...], wout_ref[...])
    gate = jax.nn.sigmoid(_dot(h, w_ref[:, 2 * RET_QK + 2 * RET_VW:]))
    o_ref[...] = ya * gate


def _mixer_a(x, cos2, sin2, g, w, nrm, wout):
    B, S, D = x.shape
    ts = min(TS_A, S)
    return pl.pallas_call(
        _mixer_a_kernel,
        grid=(B, S // ts),
        in_specs=[_seq_spec(ts, D), _seq_spec(ts, LANES), _seq_spec(ts, LANES), _const_spec(g.shape),
                  _const_spec(w.shape), _const_spec(nrm.shape), _const_spec(wout.shape)],
        out_specs=_seq_spec(ts, D),
        out_shape=jax.ShapeDtypeStruct((B, S, D), F32),
        scratch_shapes=[pltpu.VMEM((ts, D), BF16), pltpu.VMEM((ts, RET_VW), BF16),
                        pltpu.VMEM((RET_HEADS, RET_DK, RET_DV), F32)],
        compiler_params=pltpu.CompilerParams(dimension_semantics=("arbitrary", "arbitrary"),
                                             vmem_limit_bytes=VMEM_LIMIT),
        name="mixer_a",
    )(x, cos2, sin2, g, w, nrm, wout)


def _mixer_b_kernel(x_ref, m_ref, g_ref, w_ref, wa2_ref, ba_ref, nrm_ref, wout_ref, o_ref,
                    h_scr, qi_scr, ki_scr, qe_scr, ks_scr, sc_scr, vb_scr, vt_scr, dec_scr, og_scr, kv_scr,
                    sp_scr, o_scr, st_ref):
    C = GLA_CHUNK
    ts = x_ref.shape[0]

    @pl.when(pl.program_id(1) == 0)
    def _():
        st_ref[...] = jnp.zeros_like(st_ref)

    h_scr[...] = _rmsnorm(x_ref[...], g_ref[...]).astype(BF16)
    h = h_scr[...]

    o_ga = 2 * GLA_QK + 2 * GLA_VW + D_MODEL
    ga = _dot(h, w_ref[:, o_ga:o_ga + LANES])
    z = _dot(ga.astype(BF16), wa2_ref[...]) + ba_ref[...]
    la = (jnp.minimum(z, 0.0) - jnp.log1p(jnp.exp(-jnp.abs(z)))) * (1.0 / GLA_TAU)

    ii = lax.broadcasted_iota(I32, (C, C), 0)
    jj = lax.broadcasted_iota(I32, (C, C), 1)
    causal = ii >= jj
    tril = causal.astype(F32)

    q_all = _dot(h, w_ref[:, 0:GLA_QK]) * (GLA_DK ** -0.5)
    k_all = _dot(h, w_ref[:, GLA_QK:2 * GLA_QK])

    for c in range(ts // C):
        sl = slice(c * C, (c + 1) * C)
        b = _dot_hi(tril, la[sl])
        b_mid = b[C // 2:C // 2 + 1]
        b_last = b[C - 1:C]
        qc, kc = q_all[sl], k_all[sl]
        qi_scr[sl, :] = (qc * jnp.exp(b - b_mid)).astype(BF16)
        ki_scr[sl, :] = (kc * jnp.exp(b_mid - b)).astype(BF16)
        qe_scr[sl, :] = (qc * jnp.exp(b)).astype(BF16)
        ks = (kc * jnp.exp(b_last - b)).astype(BF16)
        ks_scr[c % 2, sl, :] = ks
        ks_scr[1 - c % 2, sl, :] = jnp.zeros_like(ks)
        dec_scr[c:c + 1, :] = jnp.exp(b_last)

    for hd in range(GLA_HEADS):
        v = _dot(h, w_ref[:, 2 * GLA_QK + hd * GLA_DV:2 * GLA_QK + (hd + 1) * GLA_DV])
        vb_scr[:, hd * GLA_DV:(hd + 1) * GLA_DV] = v.astype(BF16)
        vt_scr[hd * GLA_DV:(hd + 1) * GLA_DV, :] = v.T.astype(BF16)

    for hd in range(GLA_HEADS):
        hs = slice(hd * GLA_DK, (hd + 1) * GLA_DK)
        for c in range(ts // C):
            sl = slice(c * C, (c + 1) * C)
            sc_scr[hd, sl, :] = jnp.where(causal, _dot_nt(qi_scr[sl, hs], ki_scr[sl, hs]), 0.0).astype(BF16)

    for hd in range(GLA_HEADS):
        vb = vb_scr[:, hd * GLA_DV:(hd + 1) * GLA_DV]
        vt = vt_scr[hd * GLA_DV:(hd + 1) * GLA_DV, :]
        hs = slice(hd * GLA_DK, (hd + 1) * GLA_DK)
        for c in range(ts // C):
            sl = slice(c * C, (c + 1) * C)
            pr = slice(c // 2 * 2 * C, (c // 2 + 1) * 2 * C)
            kv_scr[c] = _dot(vt[:, pr], ks_scr[c % 2, pr, hs])
            og_scr[sl, :] = _dot(sc_scr[hd, sl, :], vb[sl])
        st = st_ref[hd]
        for c in range(ts // C):
            sp_scr[c] = st.astype(BF16)
            st = dec_scr[c:c + 1, hs] * st + kv_scr[c]
        st_ref[hd] = st
        for c in range(ts // C):
            sl = slice(c * C, (c + 1) * C)
            og_scr[sl, :] += _dot_nt(qe_scr[sl, hs], sp_scr[c])
        o = og_scr[...]
        y = o * lax.rsqrt(jnp.mean(o * o, axis=-1, keepdims=True) + EPS) * nrm_ref[:, hd * GLA_DV:(hd + 1) * GLA_DV]
        gr = _dot(h, w_ref[:, 2 * GLA_QK + GLA_VW + hd * GLA_DV:2 * GLA_QK + GLA_VW + (hd + 1) * GLA_DV])
        o_scr[:, hd * GLA_DV:(hd + 1) * GLA_DV] = (y * _silu(gr)).astype(BF16)

    yb = _dot(o_scr[...], wout_ref[...])
    gate = jax.nn.sigmoid(_dot(h, w_ref[:, 2 * GLA_QK + 2 * GLA_VW:2 * GLA_QK + 2 * GLA_VW + D_MODEL]))
    o_ref[...] = m_ref[...] + yb * gate


def _mixer_b(x, m, g, w, wa2, ba, nrm, wout):
    B, S, D = x.shape
    ts = min(TS_B, S)
    return pl.pallas_call(
        _mixer_b_kernel,
        grid=(B, S // ts),
        in_specs=[_seq_spec(ts, D), _seq_spec(ts, D), _const_spec(g.shape), _const_spec(w.shape),
                  _const_spec(wa2.shape), _const_spec(ba.shape), _const_spec(nrm.shape),
                  _const_spec(wout.shape)],
        out_specs=_seq_spec(ts, D),
        out_shape=jax.ShapeDtypeStruct((B, S, D), F32),
        scratch_shapes=[pltpu.VMEM((ts, D), BF16)] + [pltpu.VMEM((ts, GLA_QK), BF16)] * 3 + [
                        pltpu.VMEM((2, ts, GLA_QK), BF16),
                        pltpu.VMEM((GLA_HEADS, ts, GLA_CHUNK), BF16),
                        pltpu.VMEM((ts, GLA_VW), BF16),
                        pltpu.VMEM((GLA_VW, ts), BF16),
                        pltpu.VMEM((max(ts // GLA_CHUNK, 8), GLA_QK), F32),
                        pltpu.VMEM((ts, GLA_DV), F32),
                        pltpu.VMEM((ts // GLA_CHUNK, GLA_DV, GLA_DK), F32),
                        pltpu.VMEM((ts // GLA_CHUNK, GLA_DV, GLA_DK), BF16),
                        pltpu.VMEM((ts, GLA_VW), BF16),
                        pltpu.VMEM((GLA_HEADS, GLA_DV, GLA_DK), F32)],
        compiler_params=pltpu.CompilerParams(dimension_semantics=("arbitrary", "arbitrary"),
                                             vmem_limit_bytes=VMEM_LIMIT),
        name="mixer_b",
    )(x, m, g, w, wa2, ba, nrm, wout)


def _mixer_c_kernel(x_ref, m_ref, g_ref, w_ref, cw_ref, cb_ref, dtb_ref, a_ref, dx_ref, nrm_ref,
                    wout_ref, wo_ref, o_ref,
                    h_scr, cbuf, xs_scr, bc_scr, z_scr, dt_scr, y_scr, st_ref):
    C = SSD_CHUNK
    N = SSD_STATE
    P = SSD_HEADDIM
    ts = x_ref.shape[0]
    GN = SSD_GROUPS * N

    @pl.when(pl.program_id(1) == 0)
    def _():
        st_ref[...] = jnp.zeros_like(st_ref)
        cbuf[...] = jnp.zeros_like(cbuf)

    h_scr[...] = _rmsnorm(x_ref[...], g_ref[...]).astype(BF16)
    h = h_scr[...]

    z_scr[...] = _silu(_dot(h, w_ref[:, 0:SSD_INNER]))
    o_gt = SSD_INNER + SSD_CONV_DIM
    o_dt = o_gt + D_MODEL
    dt_scr[...] = jax.nn.softplus(_dot(h, w_ref[:, o_dt:o_dt + LANES]) + dtb_ref[...])

    CB = 512
    for cblk in range(SSD_CONV_DIM // CB):
        cs = slice(cblk * CB, (cblk + 1) * CB)
        proj = _dot(h, w_ref[:, SSD_INNER + cblk * CB:SSD_INNER + (cblk + 1) * CB])
        ext = jnp.concatenate([cbuf[:, cs], proj], axis=0)
        acc = cb_ref[:, cs] + proj * cw_ref[SSD_CONV - 1:SSD_CONV, cs]
        for k in range(1, SSD_CONV):
            j = SSD_CONV - 1 - k
            acc = acc + pltpu.roll(ext, k, 0)[CONV_PAD:] * cw_ref[j:j + 1, cs]
        act = _silu(acc)
        if cblk * CB < SSD_INNER:
            xs_scr[:, cs] = act
        else:
            bc_scr[:, cblk * CB - SSD_INNER:(cblk + 1) * CB - SSD_INNER] = act.astype(BF16)
        cbuf[:, cs] = proj[ts - CONV_PAD:]

    ii = lax.broadcasted_iota(I32, (C, C), 0)
    jj = lax.broadcasted_iota(I32, (C, C), 1)
    causal = ii >= jj
    tril = causal.astype(F32)
    triu = (ii <= jj).astype(F32)
    a_row = a_ref[...]

    def chunk_body(c, carry):
        r0 = pl.multiple_of(c * C, C)
        rs = pl.ds(r0, C)
        dt = dt_scr[rs, :]
        dta = dt * a_row
        acs = _dot_hi(tril, dta)
        acs_t = _dot_tn_hi(dta, triu)
        last_row = acs[C - 1:C, :]
        dt_t = dt.T
        w1_t = (dt * jnp.exp(last_row - acs)).T
        dec = jnp.exp(last_row)
        for g in range(SSD_GROUPS):
            bcg = bc_scr[rs, g * N:(g + 1) * N]
            ccg = bc_scr[rs, GN + g * N:GN + (g + 1) * N]
            cb = _dot_nt(ccg, bcg)
            ccf = ccg.astype(F32)
            bct = bcg.astype(F32).T
            for r in range(SSD_HEADS_PER_GROUP):
                hh = g * SSD_HEADS_PER_GROUP + r
                hc = slice(hh, hh + 1)
                col = jnp.broadcast_to(acs[:, hc], (C, C))
                lmat = jnp.exp(jnp.where(causal, col - acs_t[hc, :], -jnp.inf))
                a1 = (lmat * cb * dt_t[hc, :]).astype(BF16)
                a2 = (jnp.exp(col) * ccf).astype(BF16)
                xhb = xs_scr[rs, hh * P:(hh + 1) * P].astype(BF16)
                st = st_ref[hh]
                y_scr[rs, hh * P:(hh + 1) * P] = _dot(a1, xhb) + _dot(a2, st.astype(BF16))
                st_ref[hh] = dec[:, hc] * st + _dot((bct * w1_t[hc, :]).astype(BF16), xhb)
        return carry

    lax.fori_loop(0, ts // C, chunk_body, 0)

    GW = SSD_INNER // SSD_GROUPS
    for g in range(SSD_GROUPS):
        gs = slice(g * GW, (g + 1) * GW)
        y = (y_scr[:, gs] + xs_scr[:, gs] * dx_ref[:, gs]) * z_scr[:, gs]
        y = y * lax.rsqrt(jnp.mean(y * y, axis=-1, keepdims=True) + EPS) * nrm_ref[:, gs]
        y_scr[:, gs] = y
    yc = _dot(y_scr[...].astype(BF16), wout_ref[...])
    gate = jax.nn.sigmoid(_dot(h, w_ref[:, o_gt:o_gt + D_MODEL]))
    merged = m_ref[...] + yc * gate
    o_ref[...] = x_ref[...] + _dot(merged.astype(BF16), wo_ref[...])


def _mixer_c(x, m, g, w, cw, cb, dtb, a_row, dx, nrm, wout, wo):
    B, S, D = x.shape
    ts = min(TS_C, S)
    consts = (g, w, cw, cb, dtb, a_row, dx, nrm, wout, wo)
    return pl.pallas_call(
        _mixer_c_kernel,
        grid=(B, S // ts),
        in_specs=[_seq_spec(ts, D), _seq_spec(ts, D)] + [_const_spec(c.shape) for c in consts],
        out_specs=_seq_spec(ts, D),
        out_shape=jax.ShapeDtypeStruct((B, S, D), F32),
        scratch_shapes=[pltpu.VMEM((ts, D), BF16),
                        pltpu.VMEM((CONV_PAD, SSD_CONV_DIM), F32),
                        pltpu.VMEM((ts, SSD_INNER), F32),
                        pltpu.VMEM((ts, 2 * SSD_GROUPS * SSD_STATE), BF16),
                        pltpu.VMEM((ts, SSD_INNER), F32),
                        pltpu.VMEM((ts, LANES), F32),
                        pltpu.VMEM((ts, SSD_INNER), F32),
                        pltpu.VMEM((SSD_HEADS, SSD_STATE, SSD_HEADDIM), F32)],
        compiler_params=pltpu.CompilerParams(dimension_semantics=("arbitrary", "arbitrary"),
                                             vmem_limit_bytes=VMEM_LIMIT),
        name="mixer_c",
    )(x, m, *consts)


ROUTER_ROWS = 8 + MOE_EXPERTS


def _router_kernel(x_ref, g_ref, wr_ref, br_ref, hf_ref, mi_ref, mf_ref, cnt_ref, cnt_scr):
    tm = x_ref.shape[0]
    E = MOE_EXPERTS
    NEG = -jnp.inf

    @pl.when(pl.program_id(0) == 0)
    def _():
        cnt_scr[...] = jnp.zeros_like(cnt_scr)

    hf = _rmsnorm(x_ref[...], g_ref[...])
    hf_ref[...] = hf
    logits = _dot_nt_hi(wr_ref[...], hf) + br_ref[...]

    sub8 = lax.broadcasted_iota(I32, (8, tm), 0)
    gl = jnp.where(sub8 < MOE_GROUPS, logits[0:8], NEG)
    gmax = jnp.max(gl, axis=0, keepdims=True)
    g_sel = jnp.min(jnp.where(gl == gmax, sub8, 8), axis=0, keepdims=True)
    p_sel = 1.0 / jnp.sum(jnp.exp(gl - gmax), axis=0, keepdims=True)

    el = jnp.zeros((MOE_PER_GROUP, tm), F32)
    for g in range(MOE_GROUPS):
        el = jnp.where(g_sel == g, logits[8 + g * MOE_PER_GROUP:8 + (g + 1) * MOE_PER_GROUP], el)
    m1 = jnp.max(el, axis=0, keepdims=True)
    i1 = jnp.min(jnp.where(el == m1, sub8, 8), axis=0, keepdims=True)
    el2 = jnp.where(sub8 == i1, NEG, el)
    m2 = jnp.max(el2, axis=0, keepdims=True)
    i2 = jnp.min(jnp.where(el2 == m2, sub8, 8), axis=0, keepdims=True)
    e2 = jnp.exp(m2 - m1)
    w0 = p_sel / (1.0 + e2)
    w1 = p_sel * e2 / (1.0 + e2)
    eid0 = g_sel * MOE_PER_GROUP + i1
    eid1 = g_sel * MOE_PER_GROUP + i2

    subE = lax.broadcasted_iota(I32, (E, tm), 0)
    oh0 = subE == eid0
    oh1 = subE == eid1
    oh = (oh0 | oh1).astype(BF16)
    t_r = lax.broadcasted_iota(I32, (tm, tm), 0)
    t_c = lax.broadcasted_iota(I32, (tm, tm), 1)
    before = (t_r < t_c).astype(BF16)
    tot = _dot(oh, before) + cnt_scr[:, 0:1]
    rank0 = jnp.sum(jnp.where(oh0, tot, 0.0), axis=0, keepdims=True)
    rank1 = jnp.sum(jnp.where(oh1, tot, 0.0), axis=0, keepdims=True)
    cnt_new = cnt_scr[...] + jnp.sum(oh.astype(F32), axis=1, keepdims=True)
    cnt_scr[...] = cnt_new
    cnt_ref[...] = cnt_new.astype(I32)

    mi_ref[...] = jnp.zeros_like(mi_ref)
    mi_ref[0:1, :] = eid0
    mi_ref[1:2, :] = eid1
    mi_ref[2:3, :] = rank0.astype(I32)
    mi_ref[3:4, :] = rank1.astype(I32)
    mf_ref[...] = jnp.zeros_like(mf_ref)
    mf_ref[0:1, :] = w0
    mf_ref[1:2, :] = w1


def _router(x2, g, wr, br):
    T, D = x2.shape
    tm = min(TM_ROUTER, T)
    return pl.pallas_call(
        _router_kernel,
        grid=(T // tm,),
        in_specs=[pl.BlockSpec((tm, D), lambda i: (i, 0)), _const_spec(g.shape), _const_spec(wr.shape),
                  _const_spec(br.shape)],
        out_specs=[pl.BlockSpec((tm, D), lambda i: (i, 0)),
                   pl.BlockSpec((8, tm), lambda i: (0, i)),
                   pl.BlockSpec((8, tm), lambda i: (0, i)),
                   pl.BlockSpec((MOE_EXPERTS, LANES), lambda i: (0, 0))],
        out_shape=[jax.ShapeDtypeStruct((T, D), F32),
                   jax.ShapeDtypeStruct((8, T), I32),
                   jax.ShapeDtypeStruct((8, T), F32),
                   jax.ShapeDtypeStruct((MOE_EXPERTS, LANES), I32)],
        scratch_shapes=[pltpu.VMEM((MOE_EXPERTS, LANES), F32)],
        compiler_params=pltpu.CompilerParams(dimension_semantics=("arbitrary",),
                                             vmem_limit_bytes=VMEM_LIMIT),
        name="moe_router",
    )(x2, g, wr, br)


def _row_copy(src_ref, src_row, dst_ref, dst_row, sem):
    return pltpu.make_async_copy(src_ref.at[pl.ds(src_row, 1)], dst_ref.at[pl.ds(dst_row, 1)], sem)


def _dest_kernel(ps_ref, mi_ref, o_ref):
    nt, _, tm2 = o_ref.shape
    tm = tm2 // 2
    eid = mi_ref[0:2, :]
    d = mi_ref[2:4, :]
    for e in range(MOE_EXPERTS):
        d = d + jnp.where(eid == e, ps_ref[e], 0)
    for j in range(nt):
        o_ref[j, :, 0:tm] = d[0:1, j * tm:(j + 1) * tm]
        o_ref[j, :, tm:tm2] = d[1:2, j * tm:(j + 1) * tm]


def _dest_slots(pstarts, mi, tm):
    T = mi.shape[1]
    nt = min(8, T // tm)
    return pl.pallas_call(
        _dest_kernel,
        grid_spec=pltpu.PrefetchScalarGridSpec(
            num_scalar_prefetch=1,
            grid=(T // (nt * tm),),
            in_specs=[pl.BlockSpec((8, nt * tm), lambda i, ps: (0, i))],
            out_specs=pl.BlockSpec((nt, 1, 2 * tm), lambda i, ps: (i, 0, 0))),
        out_shape=jax.ShapeDtypeStruct((T // tm, 1, 2 * tm), I32),
        compiler_params=pltpu.CompilerParams(dimension_semantics=("arbitrary",)),
        name="moe_dest",
    )(pstarts, mi)


def _dispatch_kernel(pends_ref, dest_ref, hf_ref, xbuf_ref, zero_scr, sem):
    tm = hf_ref.shape[0]
    bm = zero_scr.shape[0]

    @pl.when(pl.program_id(0) == 0)
    def _():
        zero_scr[...] = jnp.zeros_like(zero_scr)

        def last_block(e):
            start = pl.multiple_of(jnp.maximum(pends_ref[e] - bm, 0), bm)
            return xbuf_ref.at[pl.ds(start, bm)]

        for e in range(MOE_EXPERTS):
            pltpu.make_async_copy(zero_scr, last_block(e), sem).start()
        for e in range(MOE_EXPERTS):
            pltpu.make_async_copy(zero_scr, last_block(e), sem).wait()

        def tail_block(b):
            return xbuf_ref.at[pl.ds(pl.multiple_of(b * bm, bm), bm)]

        def zero_start(b, c):
            pltpu.make_async_copy(zero_scr, tail_block(b), sem).start()
            return c

        def zero_wait(b, c):
            pltpu.make_async_copy(zero_scr, tail_block(b), sem).wait()
            return c

        first_unused = pends_ref[MOE_EXPERTS - 1] // bm
        lax.fori_loop(first_unused, xbuf_ref.shape[0] // bm, zero_start, 0)
        lax.fori_loop(first_unused, xbuf_ref.shape[0] // bm, zero_wait, 0)

    def issue(j, c):
        for u in range(MOVE_UNROLL):
            r = j * MOVE_UNROLL + u
            _row_copy(hf_ref, r, xbuf_ref, dest_ref[0, 0, r], sem).start(priority=0)
            _row_copy(hf_ref, r, xbuf_ref, dest_ref[0, 0, tm + r], sem).start(priority=1)
        return c

    lax.fori_loop(0, tm // MOVE_UNROLL, issue, 0)
    for _ in range(MOE_TOPK):
        pltpu.make_async_copy(hf_ref, xbuf_ref.at[pl.ds(0, tm)], sem).wait()


def _dispatch(hf, dest3, pends, n_rows):
    T, D = hf.shape
    tm = dest3.shape[2] // 2
    return pl.pallas_call(
        _dispatch_kernel,
        grid_spec=pltpu.PrefetchScalarGridSpec(
            num_scalar_prefetch=1,
            grid=(T // tm,),
            in_specs=[pl.BlockSpec((1, 1, 2 * tm), lambda i, pe: (i, 0, 0), memory_space=pltpu.SMEM),
                      pl.BlockSpec((tm, D), lambda i, pe: (i, 0))],
            out_specs=pl.BlockSpec(memory_space=pl.ANY),
            scratch_shapes=[pltpu.VMEM((BM_EXPERT, D), F32), pltpu.SemaphoreType.DMA(())]),
        out_shape=jax.ShapeDtypeStruct((n_rows, D), F32),
        compiler_params=pltpu.CompilerParams(dimension_semantics=("arbitrary",),
                                             vmem_limit_bytes=VMEM_LIMIT),
        name="moe_dispatch",
    )(pends, dest3, hf)


def _expert_kernel(blk_e_ref, nb_ref, x_ref, wg_ref, wu_ref, wd_ref, y_ref):
    del blk_e_ref

    used = pl.program_id(0) < nb_ref[0]

    @pl.when(used)
    def _():
        xb = x_ref[...].astype(BF16)
        a = _dot(xb, wg_ref[...].astype(BF16))
        u = _dot(xb, wu_ref[...].astype(BF16))
        y_ref[...] = _dot((_silu(a) * u).astype(BF16), wd_ref[...].astype(BF16))

    @pl.when(jnp.logical_not(used))
    def _():
        y_ref[...] = jnp.zeros_like(y_ref)


def _experts(xbuf, blk_e, nb_used, wg, wu, wd):
    P, D = xbuf.shape
    bm = BM_EXPERT
    FF = wg.shape[-1]

    def row_map(i, be, nb):
        return (i, 0)

    def used_row_map(i, be, nb):
        return (jnp.minimum(i, nb[0] - 1), 0)

    def w_map(i, be, nb):
        return (be[jnp.minimum(i, nb[0] - 1)], 0, 0)

    return pl.pallas_call(
        _expert_kernel,
        grid_spec=pltpu.PrefetchScalarGridSpec(
            num_scalar_prefetch=2,
            grid=(P // bm,),
            in_specs=[pl.BlockSpec((bm, D), used_row_map),
                      pl.BlockSpec((None, D, FF), w_map),
                      pl.BlockSpec((None, D, FF), w_map),
                      pl.BlockSpec((None, FF, D), w_map)],
            out_specs=pl.BlockSpec((bm, D), row_map)),
        out_shape=jax.ShapeDtypeStruct((P, D), F32),
        compiler_params=pltpu.CompilerParams(dimension_semantics=("arbitrary",),
                                             vmem_limit_bytes=VMEM_LIMIT),
        name="moe_experts",
    )(blk_e, nb_used, xbuf, wg, wu, wd)


def _combine_kernel(dcur_ref, dnext_ref, x_ref, w_ref, gfin_ref, ybuf_ref, o_ref, rbuf, sem, *, final_norm):
    i = pl.program_id(0)
    n = pl.num_programs(0)
    tm = x_ref.shape[0]
    slot = i % 2

    def issue(dref, s):
        def body(j, c):
            for u in range(MOVE_UNROLL):
                r = j * MOVE_UNROLL + u
                _row_copy(ybuf_ref, dref[0, 0, r], rbuf.at[s, 0], r, sem.at[s]).start(priority=0)
                _row_copy(ybuf_ref, dref[0, 0, tm + r], rbuf.at[s, 1], r, sem.at[s]).start(priority=1)
            return c

        lax.fori_loop(0, tm // MOVE_UNROLL, body, 0)

    @pl.when(i == 0)
    def _():
        issue(dcur_ref, 0)

    @pl.when(i + 1 < n)
    def _():
        issue(dnext_ref, 1 - slot)

    for k in range(MOE_TOPK):
        pltpu.make_async_copy(ybuf_ref.at[pl.ds(0, tm)], rbuf.at[slot, k], sem.at[slot]).wait()
    w = w_ref[...]
    out = x_ref[...] + (rbuf[slot, 0] * w[:, 0:1] + rbuf[slot, 1] * w[:, 1:2])
    if final_norm:
        out = _rmsnorm(out, gfin_ref[...])
    o_ref[...] = out


def _combine(x2, ybuf, dest3, wt, gfin, final_norm):
    T, D = x2.shape
    tm = dest3.shape[2] // 2
    n = T // tm
    return pl.pallas_call(
        functools.partial(_combine_kernel, final_norm=final_norm),
        grid=(n,),
        in_specs=[pl.BlockSpec((1, 1, 2 * tm), lambda i: (i, 0, 0), memory_space=pltpu.SMEM),
                  pl.BlockSpec((1, 1, 2 * tm), lambda i: (jnp.minimum(i + 1, n - 1), 0, 0),
                               memory_space=pltpu.SMEM),
                  pl.BlockSpec((tm, D), lambda i: (i, 0)),
                  pl.BlockSpec((tm, MOE_TOPK), lambda i: (i, 0)),
                  _const_spec(gfin.shape),
                  pl.BlockSpec(memory_space=pl.ANY)],
        out_specs=pl.BlockSpec((tm, D), lambda i: (i, 0)),
        out_shape=jax.ShapeDtypeStruct((T, D), F32),
        scratch_shapes=[pltpu.VMEM((2, MOE_TOPK, tm, D), F32), pltpu.SemaphoreType.DMA((2,))],
        compiler_params=pltpu.CompilerParams(dimension_semantics=("arbitrary",),
                                             vmem_limit_bytes=VMEM_LIMIT),
        name="moe_combine",
    )(dest3, dest3, x2, wt, gfin, ybuf)


def _moe(x2, g_ffn, wr, br, wg, wu, wd, gfin, final_norm):
    T, D = x2.shape
    E = MOE_EXPERTS
    bm = BM_EXPERT
    A = T * MOE_TOPK
    n_blocks = -(-A // bm) + E
    hf, mi, mf, cnt = _router(x2, g_ffn, wr, br)
    counts = cnt[:, 0]
    pcounts = (counts + bm - 1) // bm * bm
    pends = jnp.cumsum(pcounts)
    pstarts = pends - pcounts
    blk_start = jnp.arange(n_blocks, dtype=I32)[:, None] * bm
    blk_e = jnp.minimum(jnp.sum((pends[None, :] <= blk_start).astype(I32), axis=1), E - 1)
    nb_used = jnp.maximum(pends[-1:] // bm, 1).astype(I32)
    tm = min(TM_MOVE, T)
    dest3 = _dest_slots(pstarts.astype(I32), mi, tm)
    xbuf = _dispatch(hf, dest3, pends.astype(I32), n_blocks * bm)
    ybuf = _experts(xbuf, blk_e, nb_used, wg, wu, wd)
    return _combine(x2, ybuf, dest3, mf[0:2].T, gfin, final_norm)


def kernel(x, positions, g_mix, w_in, ret_norm, w_ret_out, gla_w_a2, gla_b_a, gla_norm, w_gla_out, ssd_conv_w, ssd_conv_b, ssd_dt_bias, ssd_a_log, ssd_d, ssd_norm, w_ssd_out, w_o, g_ffn, w_rg, b_rg, w_re, b_re, w_exp_gate, w_exp_up, w_exp_down, g_final):
    B, S, D = x.shape
    L = w_in.shape[0]
    T = B * S

    def cols(a, b):
        return w_in[:, :, a:b]

    def padc(w, n):
        return jnp.pad(w, ((0, 0), (0, 0), (0, n - w.shape[-1])))

    w_a = jnp.concatenate([cols(O_RQ, O_GQ), cols(O_GTA, O_GTB)], axis=-1).astype(BF16)
    w_b = jnp.concatenate([cols(O_GQ, O_GA), cols(O_GTB, O_GTC), padc(cols(O_GA, O_SZ), LANES)],
                          axis=-1).astype(BF16)
    w_c = jnp.concatenate([cols(O_SZ, O_SDT), cols(O_GTC, O_GTC + D_MODEL), padc(cols(O_SDT, O_GTA), LANES)],
                          axis=-1).astype(BF16)
    wa2 = jnp.pad(gla_w_a2, ((0, 0), (0, LANES - GLA_RANK), (0, 0))).astype(BF16)
    padl = lambda v: jnp.pad(v, ((0, 0), (0, LANES - v.shape[-1])))[:, None, :]
    dtb = padl(ssd_dt_bias.astype(F32))
    a_row = padl(-jnp.exp(ssd_a_log.astype(F32)))
    dx = jnp.repeat(ssd_d.astype(F32), SSD_HEADDIM, axis=-1)[:, None, :]
    half = RET_DK // 2
    inv = ROPE_BASE ** (-jnp.arange(half, dtype=F32) / half)
    inv2 = jnp.concatenate([inv, inv])[None, :]
    posf = positions.astype(F32)[:, :, None]
    wr = jnp.concatenate([jnp.pad(jnp.swapaxes(w_rg, 1, 2), ((0, 0), (0, 8 - MOE_GROUPS), (0, 0))),
                          jnp.swapaxes(w_re, 1, 2)], axis=1).astype(F32)
    br = jnp.concatenate([jnp.pad(b_rg, ((0, 0), (0, 8 - MOE_GROUPS))), b_re], axis=1).astype(F32)[:, :, None]
    row = lambda v: v[:, None, :].astype(F32)
    g_mix_r, g_ffn_r = row(g_mix), row(g_ffn)
    ret_norm_r, gla_norm_r, ssd_norm_r = row(ret_norm), row(gla_norm), row(ssd_norm)
    gla_ba_r, conv_b_r = row(gla_b_a), row(ssd_conv_b)
    w_ret_o, w_gla_o = w_ret_out.astype(BF16), w_gla_out.astype(BF16)
    w_ssd_o, w_o_b = w_ssd_out.astype(BF16), w_o.astype(BF16)
    gfin = g_final[None, :].astype(F32)

    cos2, sin2 = _rope_tables(posf, inv2)
    for l in range(L):
        m = _mixer_a(x, cos2, sin2, g_mix_r[l], w_a[l], ret_norm_r[l], w_ret_o[l])
        m = _mixer_b(x, m, g_mix_r[l], w_b[l], wa2[l], gla_ba_r[l], gla_norm_r[l], w_gla_o[l])
        x = _mixer_c(x, m, g_mix_r[l], w_c[l], ssd_conv_w[l].astype(F32), conv_b_r[l], dtb[l], a_row[l],
                     dx[l], ssd_norm_r[l], w_ssd_o[l], w_o_b[l])
        x = _moe(x.reshape(T, D), g_ffn_r[l], wr[l], br[l], w_exp_gate[l], w_exp_up[l], w_exp_down[l],
                 gfin, l == L - 1).reshape(B, S, D)
    return x
```

```python
import functools
import math

import jax
import jax.numpy as jnp
from jax import lax
from jax.experimental import pallas as pl
from jax.experimental.pallas import tpu as pltpu

F32 = jnp.float32
BF16 = jnp.bfloat16
I32 = jnp.int32

D_MODEL = 1024
RET_HEADS, RET_DK, RET_DV, RET_CHUNK = 4, 128, 256, 128
GLA_HEADS, GLA_DK, GLA_DV, GLA_RANK, GLA_TAU, GLA_CHUNK = 4, 128, 256, 16, 16.0, 64
SSD_INNER, SSD_HEADDIM, SSD_GROUPS, SSD_STATE, SSD_CONV, SSD_CHUNK = 2048, 64, 4, 128, 4, 128
SSD_HEADS = SSD_INNER // SSD_HEADDIM
SSD_HEADS_PER_GROUP = SSD_HEADS // SSD_GROUPS
SSD_CONV_DIM = SSD_INNER + 2 * SSD_GROUPS * SSD_STATE
MOE_GROUPS, MOE_PER_GROUP, MOE_TOPK, MOE_FF = 4, 8, 2, 512
MOE_EXPERTS = MOE_GROUPS * MOE_PER_GROUP
ROPE_BASE = 10000.0
EPS = 1e-6
LANES = 128

RET_QK = RET_HEADS * RET_DK
RET_VW = RET_HEADS * RET_DV
GLA_QK = GLA_HEADS * GLA_DK
GLA_VW = GLA_HEADS * GLA_DV
SPLIT_SIZES = (RET_QK, RET_QK, RET_VW, RET_VW,
               GLA_QK, GLA_QK, GLA_VW, GLA_VW, GLA_RANK,
               SSD_INNER, SSD_CONV_DIM, SSD_HEADS,
               D_MODEL, D_MODEL, D_MODEL)
_OFFS = [0]
for _s in SPLIT_SIZES:
    _OFFS.append(_OFFS[-1] + _s)
(O_RQ, O_RK, O_RV, O_RG, O_GQ, O_GK, O_GV, O_GR, O_GA, O_SZ, O_SXBC, O_SDT, O_GTA, O_GTB, O_GTC, _) = _OFFS

TS_A = 512
TS_B = 512
TS_C = 256
TM_ROUTER = 512
TM_MOVE = 512
MOVE_UNROLL = 8
BM_EXPERT = 512
CONV_PAD = 8
VMEM_LIMIT = 56 * 1024 * 1024


def _dot(a, b):
    return jnp.dot(a, b, preferred_element_type=F32)


def _dot_nt(a, b):
    return lax.dot_general(a, b, (((1,), (1,)), ((), ())), preferred_element_type=F32)


def _dot_tn(a, b):
    return lax.dot_general(a, b, (((0,), (0,)), ((), ())), preferred_element_type=F32)


def _dot_hi(a, b):
    return jnp.dot(a, b, preferred_element_type=F32, precision=lax.Precision.HIGHEST)


def _dot_tn_hi(a, b):
    return lax.dot_general(a, b, (((0,), (0,)), ((), ())), preferred_element_type=F32,
                           precision=lax.Precision.HIGHEST)


def _dot_nt_hi(a, b):
    return lax.dot_general(a, b, (((1,), (1,)), ((), ())), preferred_element_type=F32,
                           precision=lax.Precision.HIGHEST)


def _rmsnorm(x, g):
    return x * lax.rsqrt(jnp.mean(x * x, axis=-1, keepdims=True) + EPS) * g


def _silu(x):
    return x * jax.nn.sigmoid(x)


def _const_spec(shape):
    nd = len(shape)
    return pl.BlockSpec(shape, lambda *_: (0,) * nd)


def _seq_spec(ts, width):
    return pl.BlockSpec((None, ts, width), lambda b, s: (b, s, 0))


def _rope_kernel(pos_ref, inv_ref, cos_ref, sin_ref):
    ang = pos_ref[...] * inv_ref[...]
    lane = lax.broadcasted_iota(I32, (1, LANES), 1)
    cos_ref[...] = jnp.cos(ang)
    sin_ref[...] = jnp.sin(ang) * jnp.where(lane < RET_DK // 2, -1.0, 1.0)


def _rope_tables(posf, inv2):
    B, S, _ = posf.shape
    ts = min(2048, S)
    return pl.pallas_call(
        _rope_kernel,
        grid=(B, S // ts),
        in_specs=[_seq_spec(ts, 1), _const_spec(inv2.shape)],
        out_specs=[_seq_spec(ts, LANES), _seq_spec(ts, LANES)],
        out_shape=[jax.ShapeDtypeStruct((B, S, LANES), F32)] * 2,
        compiler_params=pltpu.CompilerParams(dimension_semantics=("arbitrary", "arbitrary")),
        name="rope_tables",
    )(posf, inv2)


def _mixer_a_kernel(x_ref, cos_ref, sin_ref, g_ref, w_ref, nrm_ref, wout_ref, o_ref,
                    h_scr, o_scr, st_ref):
    C = RET_CHUNK
    ts = x_ref.shape[0]

    @pl.when(pl.program_id(1) == 0)
    def _():
        st_ref[...] = jnp.zeros_like(st_ref)

    h_scr[...] = _rmsnorm(x_ref[...], g_ref[...]).astype(BF16)
    h = h_scr[...]

    cos2 = cos_ref[...]
    sin2 = sin_ref[...]

    def rot(t):
        return t * cos2 + pltpu.roll(t, RET_DK // 2, 1) * sin2

    ii = lax.broadcasted_iota(I32, (C, C), 0)
    jj = lax.broadcasted_iota(I32, (C, C), 1)
    dif = (ii - jj).astype(F32)
    row = lax.broadcasted_iota(I32, (C, 1), 0).astype(F32)

    q_all = _dot(h, w_ref[:, 0:RET_QK])
    k_all = _dot(h, w_ref[:, RET_QK:2 * RET_QK])

    for hd in range(RET_HEADS):
        lg = math.log1p(-(2.0 ** (-5.0 - hd)))
        dmat = jnp.where(dif >= 0, jnp.exp(jnp.maximum(dif, 0.0) * lg), 0.0)
        qdec = jnp.exp((row + 1.0) * lg)
        kdec = jnp.exp((C - 1.0 - row) * lg)
        cdec = math.exp(C * lg)
        q = rot(q_all[:, hd * RET_DK:(hd + 1) * RET_DK])
        k = rot(k_all[:, hd * RET_DK:(hd + 1) * RET_DK]) * (RET_DK ** -0.5)
        v = _dot(h, w_ref[:, 2 * RET_QK + hd * RET_DV:2 * RET_QK + (hd + 1) * RET_DV])
        g = _dot(h, w_ref[:, 2 * RET_QK + RET_VW + hd * RET_DV:2 * RET_QK + RET_VW + (hd + 1) * RET_DV])
        nrm = nrm_ref[:, hd * RET_DV:(hd + 1) * RET_DV]
        for c in range(ts // C):
            sl = slice(c * C, (c + 1) * C)
            qc, kc = q[sl], k[sl]
            vcb = v[sl].astype(BF16)
            sc = _dot_nt(qc.astype(BF16), kc.astype(BF16)) * dmat
            st = st_ref[hd]
            o = _dot(sc.astype(BF16), vcb) + _dot((qc * qdec).astype(BF16), st.astype(BF16))
            st_ref[hd] = cdec * st + _dot_tn((kc * kdec).astype(BF16), vcb)
            cen = o - jnp.mean(o, axis=-1, keepdims=True)
            y = cen * lax.rsqrt(jnp.mean(cen * cen, axis=-1, keepdims=True) + EPS) * nrm
            o_scr[sl, hd * RET_DV:(hd + 1) * RET_DV] = (y * _silu(g[sl])).astype(BF16)

    ya = _dot(o_scr[...], wout_ref[...])
    gate = jax.nn.sigmoid(_dot(h, w_ref[:, 2 * RET_QK + 2 * RET_VW:]))
    o_ref[...] = ya * gate


def _mixer_a(x, cos2, sin2, g, w, nrm, wout):
    B, S, D = x.shape
    ts = min(TS_A, S)
    return pl.pallas_call(
        _mixer_a_kernel,
        grid=(B, S // ts),
        in_specs=[_seq_spec(ts, D), _seq_spec(ts, LANES), _seq_spec(ts, LANES), _const_spec(g.shape),
                  _const_spec(w.shape), _const_spec(nrm.shape), _const_spec(wout.shape)],
        out_specs=_seq_spec(ts, D),
        out_shape=jax.ShapeDtypeStruct((B, S, D), F32),
        scratch_shapes=[pltpu.VMEM((ts, D), BF16), pltpu.VMEM((ts, RET_VW), BF16),
                        pltpu.VMEM((RET_HEADS, RET_DK, RET_DV), F32)],
        compiler_params=pltpu.CompilerParams(dimension_semantics=("arbitrary", "arbitrary"),
                                             vmem_limit_bytes=VMEM_LIMIT),
        name="mixer_a",
    )(x, cos2, sin2, g, w, nrm, wout)


def _mixer_b_kernel(x_ref, m_ref, g_ref, w_ref, wa2_ref, ba_ref, nrm_ref, wout_ref, o_ref,
                    h_scr, qi_scr, ki_scr, qe_scr, ks_scr, sc_scr, vb_scr, vt_scr, dec_scr, og_scr, kv_scr,
                    sp_scr, o_scr, st_ref):
    C = GLA_CHUNK
    ts = x_ref.shape[0]

    @pl.when(pl.program_id(1) == 0)
    def _():
        st_ref[...] = jnp.zeros_like(st_ref)

    h_scr[...] = _rmsnorm(x_ref[...], g_ref[...]).astype(BF16)
    h = h_scr[...]

    o_ga = 2 * GLA_QK + 2 * GLA_VW + D_MODEL
    ga = _dot(h, w_ref[:, o_ga:o_ga + LANES])
    z = _dot(ga.astype(BF16), wa2_ref[...]) + ba_ref[...]
    la = (jnp.minimum(z, 0.0) - jnp.log1p(jnp.exp(-jnp.abs(z)))) * (1.0 / GLA_TAU)

    ii = lax.broadcasted_iota(I32, (C, C), 0)
    jj = lax.broadcasted_iota(I32, (C, C), 1)
    causal = ii >= jj
    tril = causal.astype(F32)

    q_all = _dot(h, w_ref[:, 0:GLA_QK]) * (GLA_DK ** -0.5)
    k_all = _dot(h, w_ref[:, GLA_QK:2 * GLA_QK])

    for c in range(ts // C):
        sl = slice(c * C, (c + 1) * C)
        b = _dot_hi(tril, la[sl])
        b_mid = b[C // 2:C // 2 + 1]
        b_last = b[C - 1:C]
        qc, kc = q_all[sl], k_all[sl]
        qi_scr[sl, :] = (qc * jnp.exp(b - b_mid)).astype(BF16)
        ki_scr[sl, :] = (kc * jnp.exp(b_mid - b)).astype(BF16)
        qe_scr[sl, :] = (qc * jnp.exp(b)).astype(BF16)
        ks = (kc * jnp.exp(b_last - b)).astype(BF16)
        ks_scr[c % 2, sl, :] = ks
        ks_scr[1 - c % 2, sl, :] = jnp.zeros_like(ks)
        dec_scr[c:c + 1, :] = jnp.exp(b_last)

    for hd in range(GLA_HEADS):
        v = _dot(h, w_ref[:, 2 * GLA_QK + hd * GLA_DV:2 * GLA_QK + (hd + 1) * GLA_DV])
        vb_scr[:, hd * GLA_DV:(hd + 1) * GLA_DV] = v.astype(BF16)
        vt_scr[hd * GLA_DV:(hd + 1) * GLA_DV, :] = v.T.astype(BF16)

    for hd in range(GLA_HEADS):
        hs = slice(hd * GLA_DK, (hd + 1) * GLA_DK)
        for c in range(ts // C):
            sl = slice(c * C, (c + 1) * C)
            sc_scr[hd, sl, :] = jnp.where(causal, _dot_nt(qi_scr[sl, hs], ki_scr[sl, hs]), 0.0).astype(BF16)

    for hd in range(GLA_HEADS):
        vb = vb_scr[:, hd * GLA_DV:(hd + 1) * GLA_DV]
        vt = vt_scr[hd * GLA_DV:(hd + 1) * GLA_DV, :]
        hs = slice(hd * GLA_DK, (hd + 1) * GLA_DK)
        for c in range(ts // C):
            sl = slice(c * C, (c + 1) * C)
            pr = slice(c // 2 * 2 * C, (c // 2 + 1) * 2 * C)
            kv_scr[c] = _dot(vt[:, pr], ks_scr[c % 2, pr, hs])
            og_scr[sl, :] = _dot(sc_scr[hd, sl, :], vb[sl])
        st = st_ref[hd]
        for c in range(ts // C):
            sp_scr[c] = st.astype(BF16)
            st = dec_scr[c:c + 1, hs] * st + kv_scr[c]
        st_ref[hd] = st
        for c in range(ts // C):
            sl = slice(c * C, (c + 1) * C)
            og_scr[sl, :] += _dot_nt(qe_scr[sl, hs], sp_scr[c])
        o = og_scr[...]
        y = o * lax.rsqrt(jnp.mean(o * o, axis=-1, keepdims=True) + EPS) * nrm_ref[:, hd * GLA_DV:(hd + 1) * GLA_DV]
        gr = _dot(h, w_ref[:, 2 * GLA_QK + GLA_VW + hd * GLA_DV:2 * GLA_QK + GLA_VW + (hd + 1) * GLA_DV])
        o_scr[:, hd * GLA_DV:(hd + 1) * GLA_DV] = (y * _silu(gr)).astype(BF16)

    yb = _dot(o_scr[...], wout_ref[...])
    gate = jax.nn.sigmoid(_dot(h, w_ref[:, 2 * GLA_QK + 2 * GLA_VW:2 * GLA_QK + 2 * GLA_VW + D_MODEL]))
    o_ref[...] = m_ref[...] + yb * gate


def _mixer_b(x, m, g, w, wa2, ba, nrm, wout):
    B, S, D = x.shape
    ts = min(TS_B, S)
    return pl.pallas_call(
        _mixer_b_kernel,
        grid=(B, S // ts),
        in_specs=[_seq_spec(ts, D), _seq_spec(ts, D), _const_spec(g.shape), _const_spec(w.shape),
                  _const_spec(wa2.shape), _const_spec(ba.shape), _const_spec(nrm.shape),
                  _const_spec(wout.shape)],
        out_specs=_seq_spec(ts, D),
        out_shape=jax.ShapeDtypeStruct((B, S, D), F32),
        scratch_shapes=[pltpu.VMEM((ts, D), BF16)] + [pltpu.VMEM((ts, GLA_QK), BF16)] * 3 + [
                        pltpu.VMEM((2, ts, GLA_QK), BF16),
                        pltpu.VMEM((GLA_HEADS, ts, GLA_CHUNK), BF16),
                        pltpu.VMEM((ts, GLA_VW), BF16),
                        pltpu.VMEM((GLA_VW, ts), BF16),
                        pltpu.VMEM((max(ts // GLA_CHUNK, 8), GLA_QK), F32),
                        pltpu.VMEM((ts, GLA_DV), F32),
                        pltpu.VMEM((ts // GLA_CHUNK, GLA_DV, GLA_DK), F32),
                        pltpu.VMEM((ts // GLA_CHUNK, GLA_DV, GLA_DK), BF16),
                        pltpu.VMEM((ts, GLA_VW), BF16),
                        pltpu.VMEM((GLA_HEADS, GLA_DV, GLA_DK), F32)],
        compiler_params=pltpu.CompilerParams(dimension_semantics=("arbitrary", "arbitrary"),
                                             vmem_limit_bytes=VMEM_LIMIT),
        name="mixer_b",
    )(x, m, g, w, wa2, ba, nrm, wout)


def _mixer_c_kernel(x_ref, m_ref, g_ref, w_hbm, cw_ref, cb_ref, dtb_ref, a_ref, dx_ref, nrm_ref,
                    wout_ref, wo_ref, o_ref,
                    w_ref, w_sem, h_scr, cbuf, xs_scr, bc_scr, z_scr, dt_scr, y_scr, st_ref):
    C = SSD_CHUNK
    N = SSD_STATE
    P = SSD_HEADDIM
    ts = x_ref.shape[0]
    GN = SSD_GROUPS * N

    @pl.when((pl.program_id(0) == 0) & (pl.program_id(1) == 0))
    def _():
        cp = pltpu.make_async_copy(w_hbm, w_ref, w_sem)
        cp.start()
        cp.wait()

    @pl.when(pl.program_id(1) == 0)
    def _():
        st_ref[...] = jnp.zeros_like(st_ref)
        cbuf[...] = jnp.zeros_like(cbuf)

    h_scr[...] = _rmsnorm(x_ref[...], g_ref[...]).astype(BF16)
    h = h_scr[...]

    z_scr[...] = _silu(_dot(h, w_ref[:, 0:SSD_INNER]))
    o_gt = SSD_INNER + SSD_CONV_DIM
    o_dt = o_gt + D_MODEL
    dt_scr[...] = jax.nn.softplus(_dot(h, w_ref[:, o_dt:o_dt + LANES]) + dtb_ref[...])

    CB = 512
    for cblk in range(SSD_CONV_DIM // CB):
        cs = slice(cblk * CB, (cblk + 1) * CB)
        proj = _dot(h, w_ref[:, SSD_INNER + cblk * CB:SSD_INNER + (cblk + 1) * CB])
        ext = jnp.concatenate([cbuf[:, cs], proj], axis=0)
        acc = cb_ref[:, cs] + proj * cw_ref[SSD_CONV - 1:SSD_CONV, cs]
        for k in range(1, SSD_CONV):
            j = SSD_CONV - 1 - k
            acc = acc + pltpu.roll(ext, k, 0)[CONV_PAD:] * cw_ref[j:j + 1, cs]
        act = _silu(acc)
        if cblk * CB < SSD_INNER:
            xs_scr[:, cs] = act
        else:
            bc_scr[:, cblk * CB - SSD_INNER:(cblk + 1) * CB - SSD_INNER] = act.astype(BF16)
        cbuf[:, cs] = proj[ts - CONV_PAD:]

    ii = lax.broadcasted_iota(I32, (C, C), 0)
    jj = lax.broadcasted_iota(I32, (C, C), 1)
    causal = ii >= jj
    tril = causal.astype(F32)
    triu = (ii <= jj).astype(F32)
    a_row = a_ref[...]

    def chunk_body(c, carry):
        r0 = pl.multiple_of(c * C, C)
        rs = pl.ds(r0, C)
        dt = dt_scr[rs, :]
        dta = dt * a_row
        acs = _dot_hi(tril, dta)
        acs_t = _dot_tn_hi(dta, triu)
        last_row = acs[C - 1:C, :]
        dt_t = dt.T
        w1_t = (dt * jnp.exp(last_row - acs)).T
        dec = jnp.exp(last_row)
        for g in range(SSD_GROUPS):
            bcg = bc_scr[rs, g * N:(g + 1) * N]
            ccg = bc_scr[rs, GN + g * N:GN + (g + 1) * N]
            cb = _dot_nt(ccg, bcg)
            ccf = ccg.astype(F32)
            bct = bcg.astype(F32).T
            for r in range(SSD_HEADS_PER_GROUP):
                hh = g * SSD_HEADS_PER_GROUP + r
                hc = slice(hh, hh + 1)
                col = jnp.broadcast_to(acs[:, hc], (C, C))
                lmat = jnp.exp(jnp.where(causal, col - acs_t[hc, :], -jnp.inf))
                a1 = (lmat * cb * dt_t[hc, :]).astype(BF16)
                a2 = (jnp.exp(col) * ccf).astype(BF16)
                xhb = xs_scr[rs, hh * P:(hh + 1) * P].astype(BF16)
                st = st_ref[hh]
                y_scr[rs, hh * P:(hh + 1) * P] = _dot(a1, xhb) + _dot(a2, st.astype(BF16))
                st_ref[hh] = dec[:, hc] * st + _dot((bct * w1_t[hc, :]).astype(BF16), xhb)
        return carry

    lax.fori_loop(0, ts // C, chunk_body, 0)

    GW = SSD_INNER // SSD_GROUPS
    for g in range(SSD_GROUPS):
        gs = slice(g * GW, (g + 1) * GW)
        y = (y_scr[:, gs] + xs_scr[:, gs] * dx_ref[:, gs]) * z_scr[:, gs]
        y = y * lax.rsqrt(jnp.mean(y * y, axis=-1, keepdims=True) + EPS) * nrm_ref[:, gs]
        y_scr[:, gs] = y
    yc = _dot(y_scr[...].astype(BF16), wout_ref[...])
    gate = jax.nn.sigmoid(_dot(h, w_ref[:, o_gt:o_gt + D_MODEL]))
    merged = m_ref[...] + yc * gate
    o_ref[...] = x_ref[...] + _dot(merged.astype(BF16), wo_ref[...])


def _mixer_c(x, m, g, w, cw, cb, dtb, a_row, dx, nrm, wout, wo):
    B, S, D = x.shape
    ts = min(TS_C, S)
    consts = (g, w, cw, cb, dtb, a_row, dx, nrm, wout, wo)
    return pl.pallas_call(
        _mixer_c_kernel,
        grid=(B, S // ts),
        in_specs=[_seq_spec(ts, D), _seq_spec(ts, D)] + [
            pl.BlockSpec(memory_space=pl.ANY) if c is w else _const_spec(c.shape) for c in consts],
        out_specs=_seq_spec(ts, D),
        out_shape=jax.ShapeDtypeStruct((B, S, D), F32),
        scratch_shapes=[pltpu.VMEM(w.shape, BF16), pltpu.SemaphoreType.DMA(()),
                        pltpu.VMEM((ts, D), BF16),
                        pltpu.VMEM((CONV_PAD, SSD_CONV_DIM), F32),
                        pltpu.VMEM((ts, SSD_INNER), F32),
                        pltpu.VMEM((ts, 2 * SSD_GROUPS * SSD_STATE), BF16),
                        pltpu.VMEM((ts, SSD_INNER), F32),
                        pltpu.VMEM((ts, LANES), F32),
                        pltpu.VMEM((ts, SSD_INNER), F32),
                        pltpu.VMEM((SSD_HEADS, SSD_STATE, SSD_HEADDIM), F32)],
        compiler_params=pltpu.CompilerParams(dimension_semantics=("arbitrary", "arbitrary"),
                                             vmem_limit_bytes=VMEM_LIMIT),
        name="mixer_c",
    )(x, m, *consts)


ROUTER_ROWS = 8 + MOE_EXPERTS


def _router_kernel(x_ref, g_ref, wr_ref, br_ref, hf_ref, mi_ref, mf_ref, cnt_ref, cnt_scr):
    tm = x_ref.shape[0]
    E = MOE_EXPERTS
    NEG = -jnp.inf

    @pl.when(pl.program_id(0) == 0)
    def _():
        cnt_scr[...] = jnp.zeros_like(cnt_scr)

    hf = _rmsnorm(x_ref[...], g_ref[...])
    hf_ref[...] = hf
    logits = _dot_nt_hi(wr_ref[...], hf) + br_ref[...]

    sub8 = lax.broadcasted_iota(I32, (8, tm), 0)
    gl = jnp.where(sub8 < MOE_GROUPS, logits[0:8], NEG)
    gmax = jnp.max(gl, axis=0, keepdims=True)
    g_sel = jnp.min(jnp.where(gl == gmax, sub8, 8), axis=0, keepdims=True)
    p_sel = 1.0 / jnp.sum(jnp.exp(gl - gmax), axis=0, keepdims=True)

    el = jnp.zeros((MOE_PER_GROUP, tm), F32)
    for g in range(MOE_GROUPS):
        el = jnp.where(g_sel == g, logits[8 + g * MOE_PER_GROUP:8 + (g + 1) * MOE_PER_GROUP], el)
    m1 = jnp.max(el, axis=0, keepdims=True)
    i1 = jnp.min(jnp.where(el == m1, sub8, 8), axis=0, keepdims=True)
    el2 = jnp.where(sub8 == i1, NEG, el)
    m2 = jnp.max(el2, axis=0, keepdims=True)
    i2 = jnp.min(jnp.where(el2 == m2, sub8, 8), axis=0, keepdims=True)
    e2 = jnp.exp(m2 - m1)
    w0 = p_sel / (1.0 + e2)
    w1 = p_sel * e2 / (1.0 + e2)
    eid0 = g_sel * MOE_PER_GROUP + i1
    eid1 = g_sel * MOE_PER_GROUP + i2

    subE = lax.broadcasted_iota(I32, (E, tm), 0)
    oh0 = subE == eid0
    oh1 = subE == eid1
    oh = (oh0 | oh1).astype(BF16)
    t_r = lax.broadcasted_iota(I32, (tm, tm), 0)
    t_c = lax.broadcasted_iota(I32, (tm, tm), 1)
    before = (t_r < t_c).astype(BF16)
    tot = _dot(oh, before) + cnt_scr[:, 0:1]
    rank0 = jnp.sum(jnp.where(oh0, tot, 0.0), axis=0, keepdims=True)
    rank1 = jnp.sum(jnp.where(oh1, tot, 0.0), axis=0, keepdims=True)
    cnt_new = cnt_scr[...] + jnp.sum(oh.astype(F32), axis=1, keepdims=True)
    cnt_scr[...] = cnt_new
    cnt_ref[...] = cnt_new.astype(I32)

    mi_ref[...] = jnp.zeros_like(mi_ref)
    mi_ref[0:1, :] = eid0
    mi_ref[1:2, :] = eid1
    mi_ref[2:3, :] = rank0.astype(I32)
    mi_ref[3:4, :] = rank1.astype(I32)
    mf_ref[...] = jnp.zeros_like(mf_ref)
    mf_ref[0:1, :] = w0
    mf_ref[1:2, :] = w1


def _router(x2, g, wr, br):
    T, D = x2.shape
    tm = min(TM_ROUTER, T)
    return pl.pallas_call(
        _router_kernel,
        grid=(T // tm,),
        in_specs=[pl.BlockSpec((tm, D), lambda i: (i, 0)), _const_spec(g.shape), _const_spec(wr.shape),
                  _const_spec(br.shape)],
        out_specs=[pl.BlockSpec((tm, D), lambda i: (i, 0)),
                   pl.BlockSpec((8, tm), lambda i: (0, i)),
                   pl.BlockSpec((8, tm), lambda i: (0, i)),
                   pl.BlockSpec((MOE_EXPERTS, LANES), lambda i: (0, 0))],
        out_shape=[jax.ShapeDtypeStruct((T, D), F32),
                   jax.ShapeDtypeStruct((8, T), I32),
                   jax.ShapeDtypeStruct((8, T), F32),
                   jax.ShapeDtypeStruct((MOE_EXPERTS, LANES), I32)],
        scratch_shapes=[pltpu.VMEM((MOE_EXPERTS, LANES), F32)],
        compiler_params=pltpu.CompilerParams(dimension_semantics=("arbitrary",),
                                             vmem_limit_bytes=VMEM_LIMIT),
        name="moe_router",
    )(x2, g, wr, br)


def _row_copy(src_ref, src_row, dst_ref, dst_row, sem):
    return pltpu.make_async_copy(src_ref.at[pl.ds(src_row, 1)], dst_ref.at[pl.ds(dst_row, 1)], sem)


def _dest_kernel(ps_ref, mi_ref, o_ref):
    nt, _, tm2 = o_ref.shape
    tm = tm2 // 2
    eid = mi_ref[0:2, :]
    d = mi_ref[2:4, :]
    for e in range(MOE_EXPERTS):
        d = d + jnp.where(eid == e, ps_ref[e], 0)
    for j in range(nt):
        o_ref[j, :, 0:tm] = d[0:1, j * tm:(j + 1) * tm]
        o_ref[j, :, tm:tm2] = d[1:2, j * tm:(j + 1) * tm]


def _dest_slots(pstarts, mi, tm):
    T = mi.shape[1]
    nt = min(8, T // tm)
    return pl.pallas_call(
        _dest_kernel,
        grid_spec=pltpu.PrefetchScalarGridSpec(
            num_scalar_prefetch=1,
            grid=(T // (nt * tm),),
            in_specs=[pl.BlockSpec((8, nt * tm), lambda i, ps: (0, i))],
            out_specs=pl.BlockSpec((nt, 1, 2 * tm), lambda i, ps: (i, 0, 0))),
        out_shape=jax.ShapeDtypeStruct((T // tm, 1, 2 * tm), I32),
        compiler_params=pltpu.CompilerParams(dimension_semantics=("arbitrary",)),
        name="moe_dest",
    )(pstarts, mi)


def _dispatch_kernel(pends_ref, dest_ref, hf_ref, xbuf_ref, zero_scr, sem):
    tm = hf_ref.shape[0]
    bm = zero_scr.shape[0]

    @pl.when(pl.program_id(0) == 0)
    def _():
        zero_scr[...] = jnp.zeros_like(zero_scr)

        def last_block(e):
            start = pl.multiple_of(jnp.maximum(pends_ref[e] - bm, 0), bm)
            return xbuf_ref.at[pl.ds(start, bm)]

        for e in range(MOE_EXPERTS):
            pltpu.make_async_copy(zero_scr, last_block(e), sem).start()
        for e in range(MOE_EXPERTS):
            pltpu.make_async_copy(zero_scr, last_block(e), sem).wait()

        def tail_block(b):
            return xbuf_ref.at[pl.ds(pl.multiple_of(b * bm, bm), bm)]

        def zero_start(b, c):
            pltpu.make_async_copy(zero_scr, tail_block(b), sem).start()
            return c

        def zero_wait(b, c):
            pltpu.make_async_copy(zero_scr, tail_block(b), sem).wait()
            return c

        first_unused = pends_ref[MOE_EXPERTS - 1] // bm
        lax.fori_loop(first_unused, xbuf_ref.shape[0] // bm, zero_start, 0)
        lax.fori_loop(first_unused, xbuf_ref.shape[0] // bm, zero_wait, 0)

    def issue(j, c):
        for u in range(MOVE_UNROLL):
            r = j * MOVE_UNROLL + u
            _row_copy(hf_ref, r, xbuf_ref, dest_ref[0, 0, r], sem).start(priority=0)
            _row_copy(hf_ref, r, xbuf_ref, dest_ref[0, 0, tm + r], sem).start(priority=1)
        return c

    lax.fori_loop(0, tm // MOVE_UNROLL, issue, 0)
    for _ in range(MOE_TOPK):
        pltpu.make_async_copy(hf_ref, xbuf_ref.at[pl.ds(0, tm)], sem).wait()


def _dispatch(hf, dest3, pends, n_rows):
    T, D = hf.shape
    tm = dest3.shape[2] // 2
    return pl.pallas_call(
        _dispatch_kernel,
        grid_spec=pltpu.PrefetchScalarGridSpec(
            num_scalar_prefetch=1,
            grid=(T // tm,),
            in_specs=[pl.BlockSpec((1, 1, 2 * tm), lambda i, pe: (i, 0, 0), memory_space=pltpu.SMEM),
                      pl.BlockSpec((tm, D), lambda i, pe: (i, 0))],
            out_specs=pl.BlockSpec(memory_space=pl.ANY),
            scratch_shapes=[pltpu.VMEM((BM_EXPERT, D), F32), pltpu.SemaphoreType.DMA(())]),
        out_shape=jax.ShapeDtypeStruct((n_rows, D), F32),
        compiler_params=pltpu.CompilerParams(dimension_semantics=("arbitrary",),
                                             vmem_limit_bytes=VMEM_LIMIT),
        name="moe_dispatch",
    )(pends, dest3, hf)


def _expert_kernel(blk_e_ref, nb_ref, x_ref, wg_ref, wu_ref, wd_ref, y_ref):
    del blk_e_ref

    used = pl.program_id(0) < nb_ref[0]

    @pl.when(used)
    def _():
        xb = x_ref[...].astype(BF16)
        a = _dot(xb, wg_ref[...].astype(BF16))
        u = _dot(xb, wu_ref[...].astype(BF16))
        y_ref[...] = _dot((_silu(a) * u).astype(BF16), wd_ref[...].astype(BF16))

    @pl.when(jnp.logical_not(used))
    def _():
        y_ref[...] = jnp.zeros_like(y_ref)


def _experts(xbuf, blk_e, nb_used, wg, wu, wd):
    P, D = xbuf.shape
    bm = BM_EXPERT
    FF = wg.shape[-1]

    def row_map(i, be, nb):
        return (i, 0)

    def used_row_map(i, be, nb):
        return (jnp.minimum(i, nb[0] - 1), 0)

    def w_map(i, be, nb):
        return (be[jnp.minimum(i, nb[0] - 1)], 0, 0)

    return pl.pallas_call(
        _expert_kernel,
        grid_spec=pltpu.PrefetchScalarGridSpec(
            num_scalar_prefetch=2,
            grid=(P // bm,),
            in_specs=[pl.BlockSpec((bm, D), used_row_map),
                      pl.BlockSpec((None, D, FF), w_map),
                      pl.BlockSpec((None, D, FF), w_map),
                      pl.BlockSpec((None, FF, D), w_map)],
            out_specs=pl.BlockSpec((bm, D), row_map)),
        out_shape=jax.ShapeDtypeStruct((P, D), F32),
        compiler_params=pltpu.CompilerParams(dimension_semantics=("arbitrary",),
                                             vmem_limit_bytes=VMEM_LIMIT),
        name="moe_experts",
    )(blk_e, nb_used, xbuf, wg, wu, wd)


def _combine_kernel(dcur_ref, dnext_ref, x_ref, w_ref, gfin_ref, ybuf_ref, o_ref, rbuf, sem, *, final_norm):
    i = pl.program_id(0)
    n = pl.num_programs(0)
    tm = x_ref.shape[0]
    slot = i % 2

    def issue(dref, s):
        def body(j, c):
            for u in range(MOVE_UNROLL):
                r = j * MOVE_UNROLL + u
                _row_copy(ybuf_ref, dref[0, 0, r], rbuf.at[s, 0], r, sem.at[s]).start(priority=0)
                _row_copy(ybuf_ref, dref[0, 0, tm + r], rbuf.at[s, 1], r, sem.at[s]).start(priority=1)
            return c

        lax.fori_loop(0, tm // MOVE_UNROLL, body, 0)

    @pl.when(i == 0)
    def _():
        issue(dcur_ref, 0)

    @pl.when(i + 1 < n)
    def _():
        issue(dnext_ref, 1 - slot)

    for k in range(MOE_TOPK):
        pltpu.make_async_copy(ybuf_ref.at[pl.ds(0, tm)], rbuf.at[slot, k], sem.at[slot]).wait()
    w = w_ref[...]
    out = x_ref[...] + (rbuf[slot, 0] * w[:, 0:1] + rbuf[slot, 1] * w[:, 1:2])
    if final_norm:
        out = _rmsnorm(out, gfin_ref[...])
    o_ref[...] = out


def _combine(x2, ybuf, dest3, wt, gfin, final_norm):
    T, D = x2.shape
    tm = dest3.shape[2] // 2
    n = T // tm
    return pl.pallas_call(
        functools.partial(_combine_kernel, final_norm=final_norm),
        grid=(n,),
        in_specs=[pl.BlockSpec((1, 1, 2 * tm), lambda i: (i, 0, 0), memory_space=pltpu.SMEM),
                  pl.BlockSpec((1, 1, 2 * tm), lambda i: (jnp.minimum(i + 1, n - 1), 0, 0),
                               memory_space=pltpu.SMEM),
                  pl.BlockSpec((tm, D), lambda i: (i, 0)),
                  pl.BlockSpec((tm, MOE_TOPK), lambda i: (i, 0)),
                  _const_spec(gfin.shape),
                  pl.BlockSpec(memory_space=pl.ANY)],
        out_specs=pl.BlockSpec((tm, D), lambda i: (i, 0)),
        out_shape=jax.ShapeDtypeStruct((T, D), F32),
        scratch_shapes=[pltpu.VMEM((2, MOE_TOPK, tm, D), F32), pltpu.SemaphoreType.DMA((2,))],
        compiler_params=pltpu.CompilerParams(dimension_semantics=("arbitrary",),
                                             vmem_limit_bytes=VMEM_LIMIT),
        name="moe_combine",
    )(dest3, dest3, x2, wt, gfin, ybuf)


def _moe(x2, g_ffn, wr, br, wg, wu, wd, gfin, final_norm):
    T, D = x2.shape
    E = MOE_EXPERTS
    bm = BM_EXPERT
    A = T * MOE_TOPK
    n_blocks = -(-A // bm) + E
    hf, mi, mf, cnt = _router(x2, g_ffn, wr, br)
    counts = cnt[:, 0]
    pcounts = (counts + bm - 1) // bm * bm
    pends = jnp.cumsum(pcounts)
    pstarts = pends - pcounts
    blk_start = jnp.arange(n_blocks, dtype=I32)[:, None] * bm
    blk_e = jnp.minimum(jnp.sum((pends[None, :] <= blk_start).astype(I32), axis=1), E - 1)
    nb_used = jnp.maximum(pends[-1:] // bm, 1).astype(I32)
    tm = min(TM_MOVE, T)
    dest3 = _dest_slots(pstarts.astype(I32), mi, tm)
    xbuf = _dispatch(hf, dest3, pends.astype(I32), n_blocks * bm)
    ybuf = _experts(xbuf, blk_e, nb_used, wg, wu, wd)
    return _combine(x2, ybuf, dest3, mf[0:2].T, gfin, final_norm)


def kernel(x, positions, g_mix, w_in, ret_norm, w_ret_out, gla_w_a2, gla_b_a, gla_norm, w_gla_out, ssd_conv_w, ssd_conv_b, ssd_dt_bias, ssd_a_log, ssd_d, ssd_norm, w_ssd_out, w_o, g_ffn, w_rg, b_rg, w_re, b_re, w_exp_gate, w_exp_up, w_exp_down, g_final):
    B, S, D = x.shape
    L = w_in.shape[0]
    T = B * S

    def cols(a, b):
        return w_in[:, :, a:b]

    def padc(w, n):
        return jnp.pad(w, ((0, 0), (0, 0), (0, n - w.shape[-1])))

    w_a = jnp.concatenate([cols(O_RQ, O_GQ), cols(O_GTA, O_GTB)], axis=-1).astype(BF16)
    w_b = jnp.concatenate([cols(O_GQ, O_GA), cols(O_GTB, O_GTC), padc(cols(O_GA, O_SZ), LANES)],
                          axis=-1).astype(BF16)
    w_c = jnp.concatenate([cols(O_SZ, O_SDT), cols(O_GTC, O_GTC + D_MODEL), padc(cols(O_SDT, O_GTA), LANES)],
                          axis=-1).astype(BF16)
    wa2 = jnp.pad(gla_w_a2, ((0, 0), (0, LANES - GLA_RANK), (0, 0))).astype(BF16)
    padl = lambda v: jnp.pad(v, ((0, 0), (0, LANES - v.shape[-1])))[:, None, :]
    dtb = padl(ssd_dt_bias.astype(F32))
    a_row = padl(-jnp.exp(ssd_a_log.astype(F32)))
    dx = jnp.repeat(ssd_d.astype(F32), SSD_HEADDIM, axis=-1)[:, None, :]
    half = RET_DK // 2
    inv = ROPE_BASE ** (-jnp.arange(half, dtype=F32) / half)
    inv2 = jnp.concatenate([inv, inv])[None, :]
    posf = positions.astype(F32)[:, :, None]
    wr = jnp.concatenate([jnp.pad(jnp.swapaxes(w_rg, 1, 2), ((0, 0), (0, 8 - MOE_GROUPS), (0, 0))),
                          jnp.swapaxes(w_re, 1, 2)], axis=1).astype(F32)
    br = jnp.concatenate([jnp.pad(b_rg, ((0, 0), (0, 8 - MOE_GROUPS))), b_re], axis=1).astype(F32)[:, :, None]
    row = lambda v: v[:, None, :].astype(F32)
    g_mix_r, g_ffn_r = row(g_mix), row(g_ffn)
    ret_norm_r, gla_norm_r, ssd_norm_r = row(ret_norm), row(gla_norm), row(ssd_norm)
    gla_ba_r, conv_b_r = row(gla_b_a), row(ssd_conv_b)
    w_ret_o, w_gla_o = w_ret_out.astype(BF16), w_gla_out.astype(BF16)
    w_ssd_o, w_o_b = w_ssd_out.astype(BF16), w_o.astype(BF16)
    gfin = g_final[None, :].astype(F32)

    cos2, sin2 = _rope_tables(posf, inv2)
    for l in range(L):
        m = _mixer_a(x, cos2, sin2, g_mix_r[l], w_a[l], ret_norm_r[l], w_ret_o[l])
        m = _mixer_b(x, m, g_mix_r[l], w_b[l], wa2[l], gla_ba_r[l], gla_norm_r[l], w_gla_o[l])
        x = _mixer_c(x, m, g_mix_r[l], w_c[l], ssd_conv_w[l].astype(F32), conv_b_r[l], dtb[l], a_row[l],
                     dx[l], ssd_norm_r[l], w_ssd_o[l], w_o_b[l])
        x = _moe(x.reshape(T, D), g_ffn_r[l], wr[l], br[l], w_exp_gate[l], w_exp_up[l], w_exp_down[l],
                 gfin, l == L - 1).reshape(B, S, D)
    return x
```

```python
import functools
import math

import jax
import jax.numpy as jnp
from jax import lax
from jax.experimental import pallas as pl
from jax.experimental.pallas import tpu as pltpu

F32 = jnp.float32
BF16 = jnp.bfloat16
I32 = jnp.int32

D_MODEL = 1024
RET_HEADS, RET_DK, RET_DV, RET_CHUNK = 4, 128, 256, 128
GLA_HEADS, GLA_DK, GLA_DV, GLA_RANK, GLA_TAU, GLA_CHUNK = 4, 128, 256, 16, 16.0, 64
SSD_INNER, SSD_HEADDIM, SSD_GROUPS, SSD_STATE, SSD_CONV, SSD_CHUNK = 2048, 64, 4, 128, 4, 128
SSD_HEADS = SSD_INNER // SSD_HEADDIM
SSD_HEADS_PER_GROUP = SSD_HEADS // SSD_GROUPS
SSD_CONV_DIM = SSD_INNER + 2 * SSD_GROUPS * SSD_STATE
MOE_GROUPS, MOE_PER_GROUP, MOE_TOPK, MOE_FF = 4, 8, 2, 512
MOE_EXPERTS = MOE_GROUPS * MOE_PER_GROUP
ROPE_BASE = 10000.0
EPS = 1e-6
LANES = 128

RET_QK = RET_HEADS * RET_DK
RET_VW = RET_HEADS * RET_DV
GLA_QK = GLA_HEADS * GLA_DK
GLA_VW = GLA_HEADS * GLA_DV
SPLIT_SIZES = (RET_QK, RET_QK, RET_VW, RET_VW,
               GLA_QK, GLA_QK, GLA_VW, GLA_VW, GLA_RANK,
               SSD_INNER, SSD_CONV_DIM, SSD_HEADS,
               D_MODEL, D_MODEL, D_MODEL)
_OFFS = [0]
for _s in SPLIT_SIZES:
    _OFFS.append(_OFFS[-1] + _s)
(O_RQ, O_RK, O_RV, O_RG, O_GQ, O_GK, O_GV, O_GR, O_GA, O_SZ, O_SXBC, O_SDT, O_GTA, O_GTB, O_GTC, _) = _OFFS

TS_A = 512
TS_B = 512
TS_C = 256
TM_ROUTER = 512
TM_MOVE = 512
MOVE_UNROLL = 8
BM_EXPERT = 512
CONV_PAD = 8
VMEM_LIMIT = 56 * 1024 * 1024


def _dot(a, b):
    return jnp.dot(a, b, preferred_element_type=F32)


def _dot_nt(a, b):
    return lax.dot_general(a, b, (((1,), (1,)), ((), ())), preferred_element_type=F32)


def _dot_tn(a, b):
    return lax.dot_general(a, b, (((0,), (0,)), ((), ())), preferred_element_type=F32)


def _dot_hi(a, b):
    return jnp.dot(a, b, preferred_element_type=F32, precision=lax.Precision.HIGHEST)


def _dot_tn_hi(a, b):
    return lax.dot_general(a, b, (((0,), (0,)), ((), ())), preferred_element_type=F32,
                           precision=lax.Precision.HIGHEST)


def _dot_nt_hi(a, b):
    return lax.dot_general(a, b, (((1,), (1,)), ((), ())), preferred_element_type=F32,
                           precision=lax.Precision.HIGHEST)


def _rmsnorm(x, g):
    return x * lax.rsqrt(jnp.mean(x * x, axis=-1, keepdims=True) + EPS) * g


def _silu(x):
    return x * jax.nn.sigmoid(x)


ROW_TILE = D_MODEL // LANES


def _load_token_tiles(ref, n):
    return jnp.concatenate([ref[pl.ds(s, n, stride=ROW_TILE), :] for s in range(ROW_TILE)], axis=1)


def _store_token_tiles(ref, rows):
    n = rows.shape[0]
    for s in range(ROW_TILE):
        ref[pl.ds(s, n, stride=ROW_TILE), :] = rows[:, s * LANES:(s + 1) * LANES]


def _const_spec(shape):
    nd = len(shape)
    return pl.BlockSpec(shape, lambda *_: (0,) * nd)


def _seq_spec(ts, width):
    return pl.BlockSpec((None, ts, width), lambda b, s: (b, s, 0))


def _rope_kernel(pos_ref, inv_ref, cos_ref, sin_ref):
    ang = pos_ref[...] * inv_ref[...]
    lane = lax.broadcasted_iota(I32, (1, LANES), 1)
    cos_ref[...] = jnp.cos(ang)
    sin_ref[...] = jnp.sin(ang) * jnp.where(lane < RET_DK // 2, -1.0, 1.0)


def _rope_tables(posf, inv2):
    B, S, _ = posf.shape
    ts = min(2048, S)
    return pl.pallas_call(
        _rope_kernel,
        grid=(B, S // ts),
        in_specs=[_seq_spec(ts, 1), _const_spec(inv2.shape)],
        out_specs=[_seq_spec(ts, LANES), _seq_spec(ts, LANES)],
        out_shape=[jax.ShapeDtypeStruct((B, S, LANES), F32)] * 2,
        compiler_params=pltpu.CompilerParams(dimension_semantics=("arbitrary", "arbitrary")),
        name="rope_tables",
    )(posf, inv2)


def _mixer_a_kernel(x_ref, cos_ref, sin_ref, g_ref, w_ref, nrm_ref, wout_ref, o_ref,
                    h_scr, o_scr, st_ref):
    C = RET_CHUNK
    ts = x_ref.shape[0]

    @pl.when(pl.program_id(1) == 0)
    def _():
        st_ref[...] = jnp.zeros_like(st_ref)

    h_scr[...] = _rmsnorm(x_ref[...], g_ref[...]).astype(BF16)
    h = h_scr[...]

    cos2 = cos_ref[...]
    sin2 = sin_ref[...]

    def rot(t):
        return t * cos2 + pltpu.roll(t, RET_DK // 2, 1) * sin2

    ii = lax.broadcasted_iota(I32, (C, C), 0)
    jj = lax.broadcasted_iota(I32, (C, C), 1)
    dif = (ii - jj).astype(F32)
    row = lax.broadcasted_iota(I32, (C, 1), 0).astype(F32)

    q_all = _dot(h, w_ref[:, 0:RET_QK])
    k_all = _dot(h, w_ref[:, RET_QK:2 * RET_QK])

    for hd in range(RET_HEADS):
        lg = math.log1p(-(2.0 ** (-5.0 - hd)))
        dmat = jnp.where(dif >= 0, jnp.exp(jnp.maximum(dif, 0.0) * lg), 0.0)
        qdec = jnp.exp((row + 1.0) * lg)
        kdec = jnp.exp((C - 1.0 - row) * lg)
        cdec = math.exp(C * lg)
        q = rot(q_all[:, hd * RET_DK:(hd + 1) * RET_DK])
        k = rot(k_all[:, hd * RET_DK:(hd + 1) * RET_DK]) * (RET_DK ** -0.5)
        v = _dot(h, w_ref[:, 2 * RET_QK + hd * RET_DV:2 * RET_QK + (hd + 1) * RET_DV])
        g = _dot(h, w_ref[:, 2 * RET_QK + RET_VW + hd * RET_DV:2 * RET_QK + RET_VW + (hd + 1) * RET_DV])
        nrm = nrm_ref[:, hd * RET_DV:(hd + 1) * RET_DV]
        for c in range(ts // C):
            sl = slice(c * C, (c + 1) * C)
            qc, kc = q[sl], k[sl]
            vcb = v[sl].astype(BF16)
            sc = _dot_nt(qc.astype(BF16), kc.astype(BF16)) * dmat
            st = st_ref[hd]
            o = _dot(sc.astype(BF16), vcb) + _dot((qc * qdec).astype(BF16), st.astype(BF16))
            st_ref[hd] = cdec * st + _dot_tn((kc * kdec).astype(BF16), vcb)
            cen = o - jnp.mean(o, axis=-1, keepdims=True)
            y = cen * lax.rsqrt(jnp.mean(cen * cen, axis=-1, keepdims=True) + EPS) * nrm
            o_scr[sl, hd * RET_DV:(hd + 1) * RET_DV] = (y * _silu(g[sl])).astype(BF16)

    ya = _dot(o_scr[...], wout_ref[...])
    gate = jax.nn.sigmoid(_dot(h, w_ref[:, 2 * RET_QK + 2 * RET_VW:]))
    o_ref[...] = ya * gate


def _mixer_a(x, cos2, sin2, g, w, nrm, wout):
    B, S, D = x.shape
    ts = min(TS_A, S)
    return pl.pallas_call(
        _mixer_a_kernel,
        grid=(B, S // ts),
        in_specs=[_seq_spec(ts, D), _seq_spec(ts, LANES), _seq_spec(ts, LANES), _const_spec(g.shape),
                  _const_spec(w.shape), _const_spec(nrm.shape), _const_spec(wout.shape)],
        out_specs=_seq_spec(ts, D),
        out_shape=jax.ShapeDtypeStruct((B, S, D), F32),
        scratch_shapes=[pltpu.VMEM((ts, D), BF16), pltpu.VMEM((ts, RET_VW), BF16),
                        pltpu.VMEM((RET_HEADS, RET_DK, RET_DV), F32)],
        compiler_params=pltpu.CompilerParams(dimension_semantics=("arbitrary", "arbitrary"),
                                             vmem_limit_bytes=VMEM_LIMIT),
        name="mixer_a",
    )(x, cos2, sin2, g, w, nrm, wout)


def _mixer_b_kernel(x_ref, m_ref, g_ref, w_ref, wa2_ref, ba_ref, nrm_ref, wout_ref, o_ref,
                    h_scr, qi_scr, ki_scr, qe_scr, ks_scr, sc_scr, vb_scr, vt_scr, dec_scr, og_scr, kv_scr,
                    sp_scr, o_scr, st_ref):
    C = GLA_CHUNK
    ts = x_ref.shape[0]

    @pl.when(pl.program_id(1) == 0)
    def _():
        st_ref[...] = jnp.zeros_like(st_ref)

    h_scr[...] = _rmsnorm(x_ref[...], g_ref[...]).astype(BF16)
    h = h_scr[...]

    o_ga = 2 * GLA_QK + 2 * GLA_VW + D_MODEL
    ga = _dot(h, w_ref[:, o_ga:o_ga + LANES])
    z = _dot(ga.astype(BF16), wa2_ref[...]) + ba_ref[...]
    la = (jnp.minimum(z, 0.0) - jnp.log1p(jnp.exp(-jnp.abs(z)))) * (1.0 / GLA_TAU)

    ii = lax.broadcasted_iota(I32, (C, C), 0)
    jj = lax.broadcasted_iota(I32, (C, C), 1)
    causal = ii >= jj
    tril = causal.astype(F32)

    q_all = _dot(h, w_ref[:, 0:GLA_QK]) * (GLA_DK ** -0.5)
    k_all = _dot(h, w_ref[:, GLA_QK:2 * GLA_QK])

    for c in range(ts // C):
        sl = slice(c * C, (c + 1) * C)
        b = _dot_hi(tril, la[sl])
        b_mid = b[C // 2:C // 2 + 1]
        b_last = b[C - 1:C]
        qc, kc = q_all[sl], k_all[sl]
        qi_scr[sl, :] = (qc * jnp.exp(b - b_mid)).astype(BF16)
        ki_scr[sl, :] = (kc * jnp.exp(b_mid - b)).astype(BF16)
        qe_scr[sl, :] = (qc * jnp.exp(b)).astype(BF16)
        ks = (kc * jnp.exp(b_last - b)).astype(BF16)
        ks_scr[c % 2, sl, :] = ks
        ks_scr[1 - c % 2, sl, :] = jnp.zeros_like(ks)
        dec_scr[c:c + 1, :] = jnp.exp(b_last)

    for hd in range(GLA_HEADS):
        v = _dot(h, w_ref[:, 2 * GLA_QK + hd * GLA_DV:2 * GLA_QK + (hd + 1) * GLA_DV])
        vb_scr[:, hd * GLA_DV:(hd + 1) * GLA_DV] = v.astype(BF16)
        vt_scr[hd * GLA_DV:(hd + 1) * GLA_DV, :] = v.T.astype(BF16)

    for hd in range(GLA_HEADS):
        hs = slice(hd * GLA_DK, (hd + 1) * GLA_DK)
        for c in range(ts // C):
            sl = slice(c * C, (c + 1) * C)
            sc_scr[hd, sl, :] = jnp.where(causal, _dot_nt(qi_scr[sl, hs], ki_scr[sl, hs]), 0.0).astype(BF16)

    for hd in range(GLA_HEADS):
        vb = vb_scr[:, hd * GLA_DV:(hd + 1) * GLA_DV]
        vt = vt_scr[hd * GLA_DV:(hd + 1) * GLA_DV, :]
        hs = slice(hd * GLA_DK, (hd + 1) * GLA_DK)
        for c in range(ts // C):
            sl = slice(c * C, (c + 1) * C)
            pr = slice(c // 2 * 2 * C, (c // 2 + 1) * 2 * C)
            kv_scr[c] = _dot(vt[:, pr], ks_scr[c % 2, pr, hs])
            og_scr[sl, :] = _dot(sc_scr[hd, sl, :], vb[sl])
        st = st_ref[hd]
        for c in range(ts // C):
            sp_scr[c] = st.astype(BF16)
            st = dec_scr[c:c + 1, hs] * st + kv_scr[c]
        st_ref[hd] = st
        for c in range(ts // C):
            sl = slice(c * C, (c + 1) * C)
            og_scr[sl, :] += _dot_nt(qe_scr[sl, hs], sp_scr[c])
        o = og_scr[...]
        y = o * lax.rsqrt(jnp.mean(o * o, axis=-1, keepdims=True) + EPS) * nrm_ref[:, hd * GLA_DV:(hd + 1) * GLA_DV]
        gr = _dot(h, w_ref[:, 2 * GLA_QK + GLA_VW + hd * GLA_DV:2 * GLA_QK + GLA_VW + (hd + 1) * GLA_DV])
        o_scr[:, hd * GLA_DV:(hd + 1) * GLA_DV] = (y * _silu(gr)).astype(BF16)

    yb = _dot(o_scr[...], wout_ref[...])
    gate = jax.nn.sigmoid(_dot(h, w_ref[:, 2 * GLA_QK + 2 * GLA_VW:2 * GLA_QK + 2 * GLA_VW + D_MODEL]))
    o_ref[...] = m_ref[...] + yb * gate


def _mixer_b(x, m, g, w, wa2, ba, nrm, wout):
    B, S, D = x.shape
    ts = min(TS_B, S)
    return pl.pallas_call(
        _mixer_b_kernel,
        grid=(B, S // ts),
        in_specs=[_seq_spec(ts, D), _seq_spec(ts, D), _const_spec(g.shape), _const_spec(w.shape),
                  _const_spec(wa2.shape), _const_spec(ba.shape), _const_spec(nrm.shape),
                  _const_spec(wout.shape)],
        out_specs=_seq_spec(ts, D),
        out_shape=jax.ShapeDtypeStruct((B, S, D), F32),
        scratch_shapes=[pltpu.VMEM((ts, D), BF16)] + [pltpu.VMEM((ts, GLA_QK), BF16)] * 3 + [
                        pltpu.VMEM((2, ts, GLA_QK), BF16),
                        pltpu.VMEM((GLA_HEADS, ts, GLA_CHUNK), BF16),
                        pltpu.VMEM((ts, GLA_VW), BF16),
                        pltpu.VMEM((GLA_VW, ts), BF16),
                        pltpu.VMEM((max(ts // GLA_CHUNK, 8), GLA_QK), F32),
                        pltpu.VMEM((ts, GLA_DV), F32),
                        pltpu.VMEM((ts // GLA_CHUNK, GLA_DV, GLA_DK), F32),
                        pltpu.VMEM((ts // GLA_CHUNK, GLA_DV, GLA_DK), BF16),
                        pltpu.VMEM((ts, GLA_VW), BF16),
                        pltpu.VMEM((GLA_HEADS, GLA_DV, GLA_DK), F32)],
        compiler_params=pltpu.CompilerParams(dimension_semantics=("arbitrary", "arbitrary"),
                                             vmem_limit_bytes=VMEM_LIMIT),
        name="mixer_b",
    )(x, m, g, w, wa2, ba, nrm, wout)


def _mixer_c_kernel(x_ref, m_ref, g_ref, w_hbm, cw_ref, cb_ref, dtb_ref, a_ref, dx_ref, nrm_ref,
                    wout_ref, wo_ref, o_ref,
                    w_ref, w_sem, h_scr, cbuf, xs_scr, bc_scr, z_scr, dt_scr, y_scr, yb_scr, st_ref):
    C = SSD_CHUNK
    N = SSD_STATE
    P = SSD_HEADDIM
    ts = x_ref.shape[0]
    GN = SSD_GROUPS * N

    @pl.when((pl.program_id(0) == 0) & (pl.program_id(1) == 0))
    def _():
        cp = pltpu.make_async_copy(w_hbm, w_ref, w_sem)
        cp.start()
        cp.wait()

    @pl.when(pl.program_id(1) == 0)
    def _():
        st_ref[...] = jnp.zeros_like(st_ref)
        cbuf[...] = jnp.zeros_like(cbuf)

    h_scr[...] = _rmsnorm(x_ref[...], g_ref[...]).astype(BF16)
    h = h_scr[...]

    CB = 256
    for zb in range(SSD_INNER // CB):
        z_scr[:, zb * CB:(zb + 1) * CB] = _silu(_dot(h, w_ref[:, zb * CB:(zb + 1) * CB]))
    o_gt = SSD_INNER + SSD_CONV_DIM
    o_dt = o_gt + D_MODEL
    dt_scr[...] = jax.nn.softplus(_dot(h, w_ref[:, o_dt:o_dt + LANES]) + dtb_ref[...])

    for cblk in range(SSD_CONV_DIM // CB):
        cs = slice(cblk * CB, (cblk + 1) * CB)
        proj = _dot(h, w_ref[:, SSD_INNER + cblk * CB:SSD_INNER + (cblk + 1) * CB])
        ext = jnp.concatenate([cbuf[:, cs], proj], axis=0)
        acc = cb_ref[:, cs] + proj * cw_ref[SSD_CONV - 1:SSD_CONV, cs]
        for k in range(1, SSD_CONV):
            j = SSD_CONV - 1 - k
            acc = acc + pltpu.roll(ext, k, 0)[CONV_PAD:] * cw_ref[j:j + 1, cs]
        act = _silu(acc)
        if cblk * CB < SSD_INNER:
            xs_scr[:, cs] = act
        else:
            bc_scr[:, cblk * CB - SSD_INNER:(cblk + 1) * CB - SSD_INNER] = act.astype(BF16)
        cbuf[:, cs] = proj[ts - CONV_PAD:]

    ii = lax.broadcasted_iota(I32, (C, C), 0)
    jj = lax.broadcasted_iota(I32, (C, C), 1)
    causal = ii >= jj
    tril = causal.astype(F32)
    triu = (ii <= jj).astype(F32)
    a_row = a_ref[...]
    first_head = lax.broadcasted_iota(I32, (1, 2 * P), 1) < P

    def chunk_body(c, carry):
        r0 = pl.multiple_of(c * C, C)
        rs = pl.ds(r0, C)
        dt = dt_scr[rs, :]
        dta = dt * a_row
        acs = _dot_hi(tril, dta)
        acs_t = _dot_tn_hi(dta, triu)
        last_row = acs[C - 1:C, :]
        dt_t = dt.T
        w1_t = (dt * jnp.exp(last_row - acs)).T
        dec = jnp.exp(last_row)
        for g in range(SSD_GROUPS):
            bcg = bc_scr[rs, g * N:(g + 1) * N]
            ccg = bc_scr[rs, GN + g * N:GN + (g + 1) * N]
            cb = _dot_nt(ccg, bcg)
            ccf = ccg.astype(F32)
            bct = bcg.astype(F32).T
            for rp in range(SSD_HEADS_PER_GROUP // 2):
                h0 = g * SSD_HEADS_PER_GROUP + 2 * rp
                pc = slice(h0 * P, (h0 + 2) * P)
                xpb = xs_scr[rs, pc].astype(BF16)
                stp = st_ref[h0 // 2]
                stpb = stp.astype(BF16)
                ys, news = [], []
                for hh in (h0, h0 + 1):
                    hc = slice(hh, hh + 1)
                    col = jnp.broadcast_to(acs[:, hc], (C, C))
                    lmat = jnp.exp(jnp.where(causal, col - acs_t[hc, :], -jnp.inf))
                    a1 = (lmat * cb * dt_t[hc, :]).astype(BF16)
                    a2 = (jnp.exp(col) * ccf).astype(BF16)
                    ys.append(_dot(a1, xpb) + _dot(a2, stpb))
                    news.append(_dot((bct * w1_t[hc, :]).astype(BF16), xpb))
                y_scr[rs, pc] = jnp.where(first_head, ys[0], ys[1])
                decp = jnp.where(first_head, dec[:, h0:h0 + 1], dec[:, h0 + 1:h0 + 2])
                st_ref[h0 // 2] = decp * stp + jnp.where(first_head, news[0], news[1])
        return carry

    lax.fori_loop(0, ts // C, chunk_body, 0)

    GW = SSD_INNER // SSD_GROUPS
    for g in range(SSD_GROUPS):
        gs = slice(g * GW, (g + 1) * GW)
        y = (y_scr[:, gs] + xs_scr[:, gs] * dx_ref[:, gs]) * z_scr[:, gs]
        y = y * lax.rsqrt(jnp.mean(y * y, axis=-1, keepdims=True) + EPS) * nrm_ref[:, gs]
        yb_scr[:, gs] = y.astype(BF16)
    yc = _dot(yb_scr[...], wout_ref[...])
    gate = jax.nn.sigmoid(_dot(h, w_ref[:, o_gt:o_gt + D_MODEL]))
    merged = m_ref[...] + yc * gate
    o_ref[...] = x_ref[...] + _dot(merged.astype(BF16), wo_ref[...])


def _mixer_c(x, m, g, w, cw, cb, dtb, a_row, dx, nrm, wout, wo):
    B, S, D = x.shape
    ts = min(TS_C, S)
    consts = (g, w, cw, cb, dtb, a_row, dx, nrm, wout, wo)
    return pl.pallas_call(
        _mixer_c_kernel,
        grid=(B, S // ts),
        in_specs=[_seq_spec(ts, D), _seq_spec(ts, D)] + [
            pl.BlockSpec(memory_space=pl.ANY) if c is w else _const_spec(c.shape) for c in consts],
        out_specs=_seq_spec(ts, D),
        out_shape=jax.ShapeDtypeStruct((B, S, D), F32),
        scratch_shapes=[pltpu.VMEM(w.shape, BF16), pltpu.SemaphoreType.DMA(()),
                        pltpu.VMEM((ts, D), BF16),
                        pltpu.VMEM((CONV_PAD, SSD_CONV_DIM), F32),
                        pltpu.VMEM((ts, SSD_INNER), F32),
                        pltpu.VMEM((ts, 2 * SSD_GROUPS * SSD_STATE), BF16),
                        pltpu.VMEM((ts, SSD_INNER), F32),
                        pltpu.VMEM((ts, LANES), F32),
                        pltpu.VMEM((ts, SSD_INNER), F32),
                        pltpu.VMEM((ts, SSD_INNER), BF16),
                        pltpu.VMEM((SSD_HEADS // 2, SSD_STATE, 2 * SSD_HEADDIM), F32)],
        compiler_params=pltpu.CompilerParams(dimension_semantics=("arbitrary", "arbitrary"),
                                             vmem_limit_bytes=VMEM_LIMIT),
        name="mixer_c",
    )(x, m, *consts)


ROUTER_ROWS = 8 + MOE_EXPERTS


def _router_kernel(x_ref, g_ref, wr_ref, br_ref, hf_ref, mi_ref, mf_ref, cnt_ref, cnt_scr):
    tm = x_ref.shape[0]
    E = MOE_EXPERTS
    NEG = -jnp.inf

    @pl.when(pl.program_id(0) == 0)
    def _():
        cnt_scr[...] = jnp.zeros_like(cnt_scr)

    hf = _rmsnorm(x_ref[...], g_ref[...])
    _store_token_tiles(hf_ref, hf)
    logits = _dot_nt_hi(wr_ref[...], hf) + br_ref[...]

    sub8 = lax.broadcasted_iota(I32, (8, tm), 0)
    gl = jnp.where(sub8 < MOE_GROUPS, logits[0:8], NEG)
    gmax = jnp.max(gl, axis=0, keepdims=True)
    g_sel = jnp.min(jnp.where(gl == gmax, sub8, 8), axis=0, keepdims=True)
    p_sel = 1.0 / jnp.sum(jnp.exp(gl - gmax), axis=0, keepdims=True)

    el = jnp.zeros((MOE_PER_GROUP, tm), F32)
    for g in range(MOE_GROUPS):
        el = jnp.where(g_sel == g, logits[8 + g * MOE_PER_GROUP:8 + (g + 1) * MOE_PER_GROUP], el)
    m1 = jnp.max(el, axis=0, keepdims=True)
    i1 = jnp.min(jnp.where(el == m1, sub8, 8), axis=0, keepdims=True)
    el2 = jnp.where(sub8 == i1, NEG, el)
    m2 = jnp.max(el2, axis=0, keepdims=True)
    i2 = jnp.min(jnp.where(el2 == m2, sub8, 8), axis=0, keepdims=True)
    e2 = jnp.exp(m2 - m1)
    w0 = p_sel / (1.0 + e2)
    w1 = p_sel * e2 / (1.0 + e2)
    eid0 = g_sel * MOE_PER_GROUP + i1
    eid1 = g_sel * MOE_PER_GROUP + i2

    subE = lax.broadcasted_iota(I32, (E, tm), 0)
    oh0 = subE == eid0
    oh1 = subE == eid1
    oh = (oh0 | oh1).astype(BF16)
    t_r = lax.broadcasted_iota(I32, (tm, tm), 0)
    t_c = lax.broadcasted_iota(I32, (tm, tm), 1)
    before = (t_r < t_c).astype(BF16)
    tot = _dot(oh, before) + cnt_scr[:, 0:1]
    rank0 = jnp.sum(jnp.where(oh0, tot, 0.0), axis=0, keepdims=True)
    rank1 = jnp.sum(jnp.where(oh1, tot, 0.0), axis=0, keepdims=True)
    cnt_new = cnt_scr[...] + jnp.sum(oh.astype(F32), axis=1, keepdims=True)
    cnt_scr[...] = cnt_new
    cnt_ref[...] = cnt_new.astype(I32)

    mi_ref[...] = jnp.zeros_like(mi_ref)
    mi_ref[0:1, :] = eid0
    mi_ref[1:2, :] = eid1
    mi_ref[2:3, :] = rank0.astype(I32)
    mi_ref[3:4, :] = rank1.astype(I32)
    mf_ref[...] = jnp.zeros_like(mf_ref)
    mf_ref[0:1, :] = w0
    mf_ref[1:2, :] = w1


def _router(x2, g, wr, br):
    T, D = x2.shape
    tm = min(TM_ROUTER, T)
    return pl.pallas_call(
        _router_kernel,
        grid=(T // tm,),
        in_specs=[pl.BlockSpec((tm, D), lambda i: (i, 0)), _const_spec(g.shape), _const_spec(wr.shape),
                  _const_spec(br.shape)],
        out_specs=[pl.BlockSpec((tm * ROW_TILE, LANES), lambda i: (i, 0)),
                   pl.BlockSpec((8, tm), lambda i: (0, i)),
                   pl.BlockSpec((8, tm), lambda i: (0, i)),
                   pl.BlockSpec((MOE_EXPERTS, LANES), lambda i: (0, 0))],
        out_shape=[jax.ShapeDtypeStruct((T * ROW_TILE, LANES), F32),
                   jax.ShapeDtypeStruct((8, T), I32),
                   jax.ShapeDtypeStruct((8, T), F32),
                   jax.ShapeDtypeStruct((MOE_EXPERTS, LANES), I32)],
        scratch_shapes=[pltpu.VMEM((MOE_EXPERTS, LANES), F32)],
        compiler_params=pltpu.CompilerParams(dimension_semantics=("arbitrary",),
                                             vmem_limit_bytes=VMEM_LIMIT),
        name="moe_router",
    )(x2, g, wr, br)


def _token_rows(ref, first, n):
    return ref.at[pl.ds(pl.multiple_of(first * ROW_TILE, ROW_TILE), n * ROW_TILE)]


def _row_copy(src_ref, src_row, dst_ref, dst_row, sem):
    return pltpu.make_async_copy(_token_rows(src_ref, src_row, 1), _token_rows(dst_ref, dst_row, 1), sem)


def _dest_kernel(ps_ref, mi_ref, o_ref):
    nt, _, tm2 = o_ref.shape
    tm = tm2 // 2
    eid = mi_ref[0:2, :]
    d = mi_ref[2:4, :]
    for e in range(MOE_EXPERTS):
        d = d + jnp.where(eid == e, ps_ref[e], 0)
    for j in range(nt):
        o_ref[j, :, 0:tm] = d[0:1, j * tm:(j + 1) * tm]
        o_ref[j, :, tm:tm2] = d[1:2, j * tm:(j + 1) * tm]


def _dest_slots(pstarts, mi, tm):
    T = mi.shape[1]
    nt = min(8, T // tm)
    return pl.pallas_call(
        _dest_kernel,
        grid_spec=pltpu.PrefetchScalarGridSpec(
            num_scalar_prefetch=1,
            grid=(T // (nt * tm),),
            in_specs=[pl.BlockSpec((8, nt * tm), lambda i, ps: (0, i))],
            out_specs=pl.BlockSpec((nt, 1, 2 * tm), lambda i, ps: (i, 0, 0))),
        out_shape=jax.ShapeDtypeStruct((T // tm, 1, 2 * tm), I32),
        compiler_params=pltpu.CompilerParams(dimension_semantics=("arbitrary",)),
        name="moe_dest",
    )(pstarts, mi)


def _dispatch_kernel(pends_ref, dest_ref, hf_ref, xbuf_ref, zero_scr, sem):
    tm = hf_ref.shape[0] // ROW_TILE
    bm = zero_scr.shape[0] // ROW_TILE
    n_blocks = xbuf_ref.shape[0] // (bm * ROW_TILE)

    @pl.when(pl.program_id(0) == 0)
    def _():
        zero_scr[...] = jnp.zeros_like(zero_scr)

        def last_block(e):
            return _token_rows(xbuf_ref, jnp.maximum(pends_ref[e] - bm, 0), bm)

        for e in range(MOE_EXPERTS):
            pltpu.make_async_copy(zero_scr, last_block(e), sem).start()
        for e in range(MOE_EXPERTS):
            pltpu.make_async_copy(zero_scr, last_block(e), sem).wait()

        def zero_start(b, c):
            pltpu.make_async_copy(zero_scr, _token_rows(xbuf_ref, b * bm, bm), sem).start()
            return c

        def zero_wait(b, c):
            pltpu.make_async_copy(zero_scr, _token_rows(xbuf_ref, b * bm, bm), sem).wait()
            return c

        first_unused = pends_ref[MOE_EXPERTS - 1] // bm
        lax.fori_loop(first_unused, n_blocks, zero_start, 0)
        lax.fori_loop(first_unused, n_blocks, zero_wait, 0)

    def issue(j, c):
        for u in range(MOVE_UNROLL):
            r = j * MOVE_UNROLL + u
            _row_copy(hf_ref, r, xbuf_ref, dest_ref[0, 0, r], sem).start(priority=0)
            _row_copy(hf_ref, r, xbuf_ref, dest_ref[0, 0, tm + r], sem).start(priority=1)
        return c

    lax.fori_loop(0, tm // MOVE_UNROLL, issue, 0)
    for _ in range(MOE_TOPK):
        pltpu.make_async_copy(hf_ref, _token_rows(xbuf_ref, 0, tm), sem).wait()


def _dispatch(hf, dest3, pends, n_rows):
    tm = dest3.shape[2] // 2
    T = dest3.shape[0] * tm
    return pl.pallas_call(
        _dispatch_kernel,
        grid_spec=pltpu.PrefetchScalarGridSpec(
            num_scalar_prefetch=1,
            grid=(T // tm,),
            in_specs=[pl.BlockSpec((1, 1, 2 * tm), lambda i, pe: (i, 0, 0), memory_space=pltpu.SMEM),
                      pl.BlockSpec((tm * ROW_TILE, LANES), lambda i, pe: (i, 0))],
            out_specs=pl.BlockSpec(memory_space=pl.ANY),
            scratch_shapes=[pltpu.VMEM((BM_EXPERT * ROW_TILE, LANES), F32), pltpu.SemaphoreType.DMA(())]),
        out_shape=jax.ShapeDtypeStruct((n_rows * ROW_TILE, LANES), F32),
        compiler_params=pltpu.CompilerParams(dimension_semantics=("arbitrary",),
                                             vmem_limit_bytes=VMEM_LIMIT),
        name="moe_dispatch",
    )(pends, dest3, hf)


def _expert_kernel(blk_e_ref, nb_ref, x_ref, wg_ref, wu_ref, wd_ref, y_ref, wg_b, wu_b, wd_b):
    i = pl.program_id(0)
    used = i < nb_ref[0]

    @pl.when(used & ((i == 0) | (blk_e_ref[i] != blk_e_ref[jnp.maximum(i - 1, 0)])))
    def _():
        wg_b[...] = wg_ref[...].astype(BF16)
        wu_b[...] = wu_ref[...].astype(BF16)
        wd_b[...] = wd_ref[...].astype(BF16)

    @pl.when(used)
    def _():
        xb = _load_token_tiles(x_ref, x_ref.shape[0] // ROW_TILE).astype(BF16)
        a = _dot(xb, wg_b[...])
        u = _dot(xb, wu_b[...])
        _store_token_tiles(y_ref, _dot((_silu(a) * u).astype(BF16), wd_b[...]))

    @pl.when(jnp.logical_not(used))
    def _():
        y_ref[...] = jnp.zeros_like(y_ref)


def _experts(xbuf, blk_e, nb_used, wg, wu, wd):
    P = xbuf.shape[0] // ROW_TILE
    D = D_MODEL
    bm = BM_EXPERT
    FF = wg.shape[-1]

    def row_map(i, be, nb):
        return (i, 0)

    def used_row_map(i, be, nb):
        return (jnp.minimum(i, nb[0] - 1), 0)

    def w_map(i, be, nb):
        return (be[jnp.minimum(i, nb[0] - 1)], 0, 0)

    return pl.pallas_call(
        _expert_kernel,
        grid_spec=pltpu.PrefetchScalarGridSpec(
            num_scalar_prefetch=2,
            grid=(P // bm,),
            in_specs=[pl.BlockSpec((bm * ROW_TILE, LANES), used_row_map),
                      pl.BlockSpec((None, D, FF), w_map),
                      pl.BlockSpec((None, D, FF), w_map),
                      pl.BlockSpec((None, FF, D), w_map)],
            out_specs=pl.BlockSpec((bm * ROW_TILE, LANES), row_map),
            scratch_shapes=[pltpu.VMEM((D, FF), BF16), pltpu.VMEM((D, FF), BF16), pltpu.VMEM((FF, D), BF16)]),
        out_shape=jax.ShapeDtypeStruct((P * ROW_TILE, LANES), F32),
        compiler_params=pltpu.CompilerParams(dimension_semantics=("arbitrary",),
                                             vmem_limit_bytes=VMEM_LIMIT),
        name="moe_experts",
    )(blk_e, nb_used, xbuf, wg, wu, wd)


def _combine_kernel(dcur_ref, dnext_ref, x_ref, w_ref, gfin_ref, ybuf_ref, o_ref, rbuf, sem, *, final_norm):
    i = pl.program_id(0)
    n = pl.num_programs(0)
    tm = x_ref.shape[0]
    slot = i % 2

    def issue(dref, s):
        def body(j, c):
            for u in range(MOVE_UNROLL):
                r = j * MOVE_UNROLL + u
                _row_copy(ybuf_ref, dref[0, 0, r], rbuf.at[s, 0], r, sem.at[s]).start(priority=0)
                _row_copy(ybuf_ref, dref[0, 0, tm + r], rbuf.at[s, 1], r, sem.at[s]).start(priority=1)
            return c

        lax.fori_loop(0, tm // MOVE_UNROLL, body, 0)

    @pl.when(i == 0)
    def _():
        issue(dcur_ref, 0)

    @pl.when(i + 1 < n)
    def _():
        issue(dnext_ref, 1 - slot)

    for k in range(MOE_TOPK):
        pltpu.make_async_copy(_token_rows(ybuf_ref, 0, tm), rbuf.at[slot, k], sem.at[slot]).wait()
    w = w_ref[...]
    r0 = _load_token_tiles(rbuf.at[slot, 0], tm)
    r1 = _load_token_tiles(rbuf.at[slot, 1], tm)
    out = x_ref[...] + (r0 * w[:, 0:1] + r1 * w[:, 1:2])
    if final_norm:
        out = _rmsnorm(out, gfin_ref[...])
    o_ref[...] = out


def _combine(x2, ybuf, dest3, wt, gfin, final_norm):
    T, D = x2.shape
    tm = dest3.shape[2] // 2
    n = T // tm
    return pl.pallas_call(
        functools.partial(_combine_kernel, final_norm=final_norm),
        grid=(n,),
        in_specs=[pl.BlockSpec((1, 1, 2 * tm), lambda i: (i, 0, 0), memory_space=pltpu.SMEM),
                  pl.BlockSpec((1, 1, 2 * tm), lambda i: (jnp.minimum(i + 1, n - 1), 0, 0),
                               memory_space=pltpu.SMEM),
                  pl.BlockSpec((tm, D), lambda i: (i, 0)),
                  pl.BlockSpec((tm, MOE_TOPK), lambda i: (i, 0)),
                  _const_spec(gfin.shape),
                  pl.BlockSpec(memory_space=pl.ANY)],
        out_specs=pl.BlockSpec((tm, D), lambda i: (i, 0)),
        out_shape=jax.ShapeDtypeStruct((T, D), F32),
        scratch_shapes=[pltpu.VMEM((2, MOE_TOPK, tm * ROW_TILE, LANES), F32), pltpu.SemaphoreType.DMA((2,))],
        compiler_params=pltpu.CompilerParams(dimension_semantics=("arbitrary",),
                                             vmem_limit_bytes=VMEM_LIMIT),
        name="moe_combine",
    )(dest3, dest3, x2, wt, gfin, ybuf)


def _moe(x2, g_ffn, wr, br, wg, wu, wd, gfin, final_norm):
    T, D = x2.shape
    E = MOE_EXPERTS
    bm = BM_EXPERT
    A = T * MOE_TOPK
    n_blocks = -(-A // bm) + E
    hf, mi, mf, cnt = _router(x2, g_ffn, wr, br)
    counts = cnt[:, 0]
    pcounts = (counts + bm - 1) // bm * bm
    pends = jnp.cumsum(pcounts)
    pstarts = pends - pcounts
    blk_start = jnp.arange(n_blocks, dtype=I32)[:, None] * bm
    blk_e = jnp.minimum(jnp.sum((pends[None, :] <= blk_start).astype(I32), axis=1), E - 1)
    nb_used = jnp.maximum(pends[-1:] // bm, 1).astype(I32)
    tm = min(TM_MOVE, T)
    dest3 = _dest_slots(pstarts.astype(I32), mi, tm)
    xbuf = _dispatch(hf, dest3, pends.astype(I32), n_blocks * bm)
    ybuf = _experts(xbuf, blk_e, nb_used, wg, wu, wd)
    return _combine(x2, ybuf, dest3, mf[0:2].T, gfin, final_norm)


def kernel(x, positions, g_mix, w_in, ret_norm, w_ret_out, gla_w_a2, gla_b_a, gla_norm, w_gla_out, ssd_conv_w, ssd_conv_b, ssd_dt_bias, ssd_a_log, ssd_d, ssd_norm, w_ssd_out, w_o, g_ffn, w_rg, b_rg, w_re, b_re, w_exp_gate, w_exp_up, w_exp_down, g_final):
    B, S, D = x.shape
    L = w_in.shape[0]
    T = B * S

    def cols(a, b):
        return w_in[:, :, a:b]

    def padc(w, n):
        return jnp.pad(w, ((0, 0), (0, 0), (0, n - w.shape[-1])))

    w_a = jnp.concatenate([cols(O_RQ, O_GQ), cols(O_GTA, O_GTB)], axis=-1).astype(BF16)
    w_b = jnp.concatenate([cols(O_GQ, O_GA), cols(O_GTB, O_GTC), padc(cols(O_GA, O_SZ), LANES)],
                          axis=-1).astype(BF16)
    w_c = jnp.concatenate([cols(O_SZ, O_SDT), cols(O_GTC, O_GTC + D_MODEL), padc(cols(O_SDT, O_GTA), LANES)],
                          axis=-1).astype(BF16)
    wa2 = jnp.pad(gla_w_a2, ((0, 0), (0, LANES - GLA_RANK), (0, 0))).astype(BF16)
    padl = lambda v: jnp.pad(v, ((0, 0), (0, LANES - v.shape[-1])))[:, None, :]
    dtb = padl(ssd_dt_bias.astype(F32))
    a_row = padl(-jnp.exp(ssd_a_log.astype(F32)))
    dx = jnp.repeat(ssd_d.astype(F32), SSD_HEADDIM, axis=-1)[:, None, :]
    half = RET_DK // 2
    inv = ROPE_BASE ** (-jnp.arange(half, dtype=F32) / half)
    inv2 = jnp.concatenate([inv, inv])[None, :]
    posf = positions.astype(F32)[:, :, None]
    wr = jnp.concatenate([jnp.pad(jnp.swapaxes(w_rg, 1, 2), ((0, 0), (0, 8 - MOE_GROUPS), (0, 0))),
                          jnp.swapaxes(w_re, 1, 2)], axis=1).astype(F32)
    br = jnp.concatenate([jnp.pad(b_rg, ((0, 0), (0, 8 - MOE_GROUPS))), b_re], axis=1).astype(F32)[:, :, None]
    row = lambda v: v[:, None, :].astype(F32)
    g_mix_r, g_ffn_r = row(g_mix), row(g_ffn)
    ret_norm_r, gla_norm_r, ssd_norm_r = row(ret_norm), row(gla_norm), row(ssd_norm)
    gla_ba_r, conv_b_r = row(gla_b_a), row(ssd_conv_b)
    w_ret_o, w_gla_o = w_ret_out.astype(BF16), w_gla_out.astype(BF16)
    w_ssd_o, w_o_b = w_ssd_out.astype(BF16), w_o.astype(BF16)
    gfin = g_final[None, :].astype(F32)

    cos2, sin2 = _rope_tables(posf, inv2)
    for l in range(L):
        m = _mixer_a(x, cos2, sin2, g_mix_r[l], w_a[l], ret_norm_r[l], w_ret_o[l])
        m = _mixer_b(x, m, g_mix_r[l], w_b[l], wa2[l], gla_ba_r[l], gla_norm_r[l], w_gla_o[l])
        x = _mixer_c(x, m, g_mix_r[l], w_c[l], ssd_conv_w[l].astype(F32), conv_b_r[l], dtb[l], a_row[l],
                     dx[l], ssd_norm_r[l], w_ssd_o[l], w_o_b[l])
        x = _moe(x.reshape(T, D), g_ffn_r[l], wr[l], br[l], w_exp_gate[l], w_exp_up[l], w_exp_down[l],
                 gfin, l == L - 1).reshape(B, S, D)
    return x
```

```python
import functools
import math

import jax
import jax.numpy as jnp
from jax import lax
from jax.experimental import pallas as pl
from jax.experimental.pallas import tpu as pltpu

F32 = jnp.float32
BF16 = jnp.bfloat16
I32 = jnp.int32

D_MODEL = 1024
RET_HEADS, RET_DK, RET_DV, RET_CHUNK = 4, 128, 256, 128
GLA_HEADS, GLA_DK, GLA_DV, GLA_RANK, GLA_TAU, GLA_CHUNK = 4, 128, 256, 16, 16.0, 64
SSD_INNER, SSD_HEADDIM, SSD_GROUPS, SSD_STATE, SSD_CONV, SSD_CHUNK = 2048, 64, 4, 128, 4, 128
SSD_HEADS = SSD_INNER // SSD_HEADDIM
SSD_HEADS_PER_GROUP = SSD_HEADS // SSD_GROUPS
SSD_CONV_DIM = SSD_INNER + 2 * SSD_GROUPS * SSD_STATE
MOE_GROUPS, MOE_PER_GROUP, MOE_TOPK, MOE_FF = 4, 8, 2, 512
MOE_EXPERTS = MOE_GROUPS * MOE_PER_GROUP
ROPE_BASE = 10000.0
EPS = 1e-6
LANES = 128

RET_QK = RET_HEADS * RET_DK
RET_VW = RET_HEADS * RET_DV
GLA_QK = GLA_HEADS * GLA_DK
GLA_VW = GLA_HEADS * GLA_DV
SPLIT_SIZES = (RET_QK, RET_QK, RET_VW, RET_VW,
               GLA_QK, GLA_QK, GLA_VW, GLA_VW, GLA_RANK,
               SSD_INNER, SSD_CONV_DIM, SSD_HEADS,
               D_MODEL, D_MODEL, D_MODEL)
_OFFS = [0]
for _s in SPLIT_SIZES:
    _OFFS.append(_OFFS[-1] + _s)
(O_RQ, O_RK, O_RV, O_RG, O_GQ, O_GK, O_GV, O_GR, O_GA, O_SZ, O_SXBC, O_SDT, O_GTA, O_GTB, O_GTC, _) = _OFFS

TS_A = 512
TS_B = 512
TS_C = 256
TM_ROUTER = 512
TM_MOVE = 512
MOVE_UNROLL = 8
BM_EXPERT = 512
CONV_PAD = 8
VMEM_LIMIT = 56 * 1024 * 1024


def _dot(a, b):
    return jnp.dot(a, b, preferred_element_type=F32)


def _dot_nt(a, b):
    return lax.dot_general(a, b, (((1,), (1,)), ((), ())), preferred_element_type=F32)


def _dot_tn(a, b):
    return lax.dot_general(a, b, (((0,), (0,)), ((), ())), preferred_element_type=F32)


def _dot_hi(a, b):
    return jnp.dot(a, b, preferred_element_type=F32, precision=lax.Precision.HIGHEST)


def _dot_tn_hi(a, b):
    return lax.dot_general(a, b, (((0,), (0,)), ((), ())), preferred_element_type=F32,
                           precision=lax.Precision.HIGHEST)


def _dot_nt_hi(a, b):
    return lax.dot_general(a, b, (((1,), (1,)), ((), ())), preferred_element_type=F32,
                           precision=lax.Precision.HIGHEST)


def _rmsnorm(x, g):
    return x * lax.rsqrt(jnp.mean(x * x, axis=-1, keepdims=True) + EPS) * g


def _silu(x):
    return x * jax.nn.sigmoid(x)


ROW_TILE = D_MODEL // LANES


def _load_token_tiles(ref, n):
    return jnp.concatenate([ref[pl.ds(s, n, stride=ROW_TILE), :] for s in range(ROW_TILE)], axis=1)


def _store_token_tiles(ref, rows):
    n = rows.shape[0]
    for s in range(ROW_TILE):
        ref[pl.ds(s, n, stride=ROW_TILE), :] = rows[:, s * LANES:(s + 1) * LANES]


def _const_spec(shape):
    nd = len(shape)
    return pl.BlockSpec(shape, lambda *_: (0,) * nd)


def _layer_spec(stacked, layer):
    nd = stacked.ndim - 1
    return pl.BlockSpec((None,) + stacked.shape[1:], lambda *_: (layer,) + (0,) * nd)


def _seq_spec(ts, width):
    return pl.BlockSpec((None, ts, width), lambda b, s: (b, s, 0))


def _rope_kernel(pos_ref, inv_ref, cos_ref, sin_ref):
    ang = pos_ref[...] * inv_ref[...]
    lane = lax.broadcasted_iota(I32, (1, LANES), 1)
    cos_ref[...] = jnp.cos(ang)
    sin_ref[...] = jnp.sin(ang) * jnp.where(lane < RET_DK // 2, -1.0, 1.0)


def _rope_tables(posf, inv2):
    B, S, _ = posf.shape
    ts = min(2048, S)
    return pl.pallas_call(
        _rope_kernel,
        grid=(B, S // ts),
        in_specs=[_seq_spec(ts, 1), _const_spec(inv2.shape)],
        out_specs=[_seq_spec(ts, LANES), _seq_spec(ts, LANES)],
        out_shape=[jax.ShapeDtypeStruct((B, S, LANES), F32)] * 2,
        compiler_params=pltpu.CompilerParams(dimension_semantics=("arbitrary", "arbitrary")),
        name="rope_tables",
    )(posf, inv2)


def _mixer_a_kernel(x_ref, cos_ref, sin_ref, g_ref, w_ref, nrm_ref, wout_ref, o_ref,
                    h_scr, o_scr, st_ref):
    C = RET_CHUNK
    ts = x_ref.shape[0]

    @pl.when(pl.program_id(1) == 0)
    def _():
        st_ref[...] = jnp.zeros_like(st_ref)

    h_scr[...] = _rmsnorm(x_ref[...], g_ref[...]).astype(BF16)
    h = h_scr[...]

    cos2 = cos_ref[...]
    sin2 = sin_ref[...]

    def rot(t):
        return t * cos2 + pltpu.roll(t, RET_DK // 2, 1) * sin2

    ii = lax.broadcasted_iota(I32, (C, C), 0)
    jj = lax.broadcasted_iota(I32, (C, C), 1)
    dif = (ii - jj).astype(F32)
    row = lax.broadcasted_iota(I32, (C, 1), 0).astype(F32)

    q_all = _dot(h, w_ref[:, 0:RET_QK])
    k_all = _dot(h, w_ref[:, RET_QK:2 * RET_QK])

    for hd in range(RET_HEADS):
        lg = math.log1p(-(2.0 ** (-5.0 - hd)))
        dmat = jnp.where(dif >= 0, jnp.exp(jnp.maximum(dif, 0.0) * lg), 0.0)
        qdec = jnp.exp((row + 1.0) * lg)
        kdec = jnp.exp((C - 1.0 - row) * lg)
        cdec = math.exp(C * lg)
        q = rot(q_all[:, hd * RET_DK:(hd + 1) * RET_DK])
        k = rot(k_all[:, hd * RET_DK:(hd + 1) * RET_DK]) * (RET_DK ** -0.5)
        v = _dot(h, w_ref[:, 2 * RET_QK + hd * RET_DV:2 * RET_QK + (hd + 1) * RET_DV])
        g = _dot(h, w_ref[:, 2 * RET_QK + RET_VW + hd * RET_DV:2 * RET_QK + RET_VW + (hd + 1) * RET_DV])
        nrm = nrm_ref[:, hd * RET_DV:(hd + 1) * RET_DV]
        for c in range(ts // C):
            sl = slice(c * C, (c + 1) * C)
            qc, kc = q[sl], k[sl]
            vcb = v[sl].astype(BF16)
            sc = _dot_nt(qc.astype(BF16), kc.astype(BF16)) * dmat
            st = st_ref[hd]
            o = _dot(sc.astype(BF16), vcb) + _dot((qc * qdec).astype(BF16), st.astype(BF16))
            st_ref[hd] = cdec * st + _dot_tn((kc * kdec).astype(BF16), vcb)
            cen = o - jnp.mean(o, axis=-1, keepdims=True)
            y = cen * lax.rsqrt(jnp.mean(cen * cen, axis=-1, keepdims=True) + EPS) * nrm
            o_scr[sl, hd * RET_DV:(hd + 1) * RET_DV] = (y * _silu(g[sl])).astype(BF16)

    ya = _dot(o_scr[...], wout_ref[...])
    gate = jax.nn.sigmoid(_dot(h, w_ref[:, 2 * RET_QK + 2 * RET_VW:]))
    o_ref[...] = ya * gate


def _mixer_a(x, cos2, sin2, g, w, nrm, wout, layer):
    B, S, D = x.shape
    ts = min(TS_A, S)
    return pl.pallas_call(
        _mixer_a_kernel,
        grid=(B, S // ts),
        in_specs=[_seq_spec(ts, D), _seq_spec(ts, LANES), _seq_spec(ts, LANES), _const_spec(g.shape),
                  _layer_spec(w, layer), _const_spec(nrm.shape), _layer_spec(wout, layer)],
        out_specs=_seq_spec(ts, D),
        out_shape=jax.ShapeDtypeStruct((B, S, D), F32),
        scratch_shapes=[pltpu.VMEM((ts, D), BF16), pltpu.VMEM((ts, RET_VW), BF16),
                        pltpu.VMEM((RET_HEADS, RET_DK, RET_DV), F32)],
        compiler_params=pltpu.CompilerParams(dimension_semantics=("arbitrary", "arbitrary"),
                                             vmem_limit_bytes=VMEM_LIMIT),
        name="mixer_a",
    )(x, cos2, sin2, g, w, nrm, wout)


def _mixer_b_kernel(x_ref, m_ref, g_ref, w_ref, wa2_ref, ba_ref, nrm_ref, wout_ref, o_ref,
                    h_scr, qi_scr, ki_scr, qe_scr, ks_scr, sc_scr, vb_scr, vt_scr, dec_scr, og_scr, kv_scr,
                    sp_scr, o_scr, st_ref):
    C = GLA_CHUNK
    ts = x_ref.shape[0]

    @pl.when(pl.program_id(1) == 0)
    def _():
        st_ref[...] = jnp.zeros_like(st_ref)

    h_scr[...] = _rmsnorm(x_ref[...], g_ref[...]).astype(BF16)
    h = h_scr[...]

    o_ga = 2 * GLA_QK + 2 * GLA_VW + D_MODEL
    ga = _dot(h, w_ref[:, o_ga:o_ga + LANES])
    z = _dot(ga.astype(BF16), wa2_ref[...]) + ba_ref[...]
    la = (jnp.minimum(z, 0.0) - jnp.log1p(jnp.exp(-jnp.abs(z)))) * (1.0 / GLA_TAU)

    ii = lax.broadcasted_iota(I32, (C, C), 0)
    jj = lax.broadcasted_iota(I32, (C, C), 1)
    causal = ii >= jj
    tril = causal.astype(F32)

    q_all = _dot(h, w_ref[:, 0:GLA_QK]) * (GLA_DK ** -0.5)
    k_all = _dot(h, w_ref[:, GLA_QK:2 * GLA_QK])

    for c in range(ts // C):
        sl = slice(c * C, (c + 1) * C)
        b = _dot_hi(tril, la[sl])
        b_mid = b[C // 2:C // 2 + 1]
        b_last = b[C - 1:C]
        qc, kc = q_all[sl], k_all[sl]
        qi_scr[sl, :] = (qc * jnp.exp(b - b_mid)).astype(BF16)
        ki_scr[sl, :] = (kc * jnp.exp(b_mid - b)).astype(BF16)
        qe_scr[sl, :] = (qc * jnp.exp(b)).astype(BF16)
        ks = (kc * jnp.exp(b_last - b)).astype(BF16)
        ks_scr[c % 2, sl, :] = ks
        ks_scr[1 - c % 2, sl, :] = jnp.zeros_like(ks)
        dec_scr[c:c + 1, :] = jnp.exp(b_last)

    for hd in range(GLA_HEADS):
        v = _dot(h, w_ref[:, 2 * GLA_QK + hd * GLA_DV:2 * GLA_QK + (hd + 1) * GLA_DV])
        vb_scr[:, hd * GLA_DV:(hd + 1) * GLA_DV] = v.astype(BF16)
        vt_scr[hd * GLA_DV:(hd + 1) * GLA_DV, :] = v.T.astype(BF16)

    for hd in range(GLA_HEADS):
        hs = slice(hd * GLA_DK, (hd + 1) * GLA_DK)
        for c in range(ts // C):
            sl = slice(c * C, (c + 1) * C)
            sc_scr[hd, sl, :] = jnp.where(causal, _dot_nt(qi_scr[sl, hs], ki_scr[sl, hs]), 0.0).astype(BF16)

    for hd in range(GLA_HEADS):
        vb = vb_scr[:, hd * GLA_DV:(hd + 1) * GLA_DV]
        vt = vt_scr[hd * GLA_DV:(hd + 1) * GLA_DV, :]
        hs = slice(hd * GLA_DK, (hd + 1) * GLA_DK)
        for c in range(ts // C):
            sl = slice(c * C, (c + 1) * C)
            pr = slice(c // 2 * 2 * C, (c // 2 + 1) * 2 * C)
            kv_scr[c] = _dot(vt[:, pr], ks_scr[c % 2, pr, hs])
            og_scr[sl, :] = _dot(sc_scr[hd, sl, :], vb[sl])
        st = st_ref[hd]
        for c in range(ts // C):
            sp_scr[c] = st.astype(BF16)
            st = dec_scr[c:c + 1, hs] * st + kv_scr[c]
        st_ref[hd] = st
        for c in range(ts // C):
            sl = slice(c * C, (c + 1) * C)
            og_scr[sl, :] += _dot_nt(qe_scr[sl, hs], sp_scr[c])
        o = og_scr[...]
        y = o * lax.rsqrt(jnp.mean(o * o, axis=-1, keepdims=True) + EPS) * nrm_ref[:, hd * GLA_DV:(hd + 1) * GLA_DV]
        gr = _dot(h, w_ref[:, 2 * GLA_QK + GLA_VW + hd * GLA_DV:2 * GLA_QK + GLA_VW + (hd + 1) * GLA_DV])
        o_scr[:, hd * GLA_DV:(hd + 1) * GLA_DV] = (y * _silu(gr)).astype(BF16)

    yb = _dot(o_scr[...], wout_ref[...])
    gate = jax.nn.sigmoid(_dot(h, w_ref[:, 2 * GLA_QK + 2 * GLA_VW:2 * GLA_QK + 2 * GLA_VW + D_MODEL]))
    o_ref[...] = m_ref[...] + yb * gate


def _mixer_b(x, m, g, w, wa2, ba, nrm, wout, layer):
    B, S, D = x.shape
    ts = min(TS_B, S)
    return pl.pallas_call(
        _mixer_b_kernel,
        grid=(B, S // ts),
        in_specs=[_seq_spec(ts, D), _seq_spec(ts, D), _const_spec(g.shape), _layer_spec(w, layer),
                  _const_spec(wa2.shape), _const_spec(ba.shape), _const_spec(nrm.shape),
                  _layer_spec(wout, layer)],
        out_specs=_seq_spec(ts, D),
        out_shape=jax.ShapeDtypeStruct((B, S, D), F32),
        scratch_shapes=[pltpu.VMEM((ts, D), BF16)] + [pltpu.VMEM((ts, GLA_QK), BF16)] * 3 + [
                        pltpu.VMEM((2, ts, GLA_QK), BF16),
                        pltpu.VMEM((GLA_HEADS, ts, GLA_CHUNK), BF16),
                        pltpu.VMEM((ts, GLA_VW), BF16),
                        pltpu.VMEM((GLA_VW, ts), BF16),
                        pltpu.VMEM((max(ts // GLA_CHUNK, 8), GLA_QK), F32),
                        pltpu.VMEM((ts, GLA_DV), F32),
                        pltpu.VMEM((ts // GLA_CHUNK, GLA_DV, GLA_DK), F32),
                        pltpu.VMEM((ts // GLA_CHUNK, GLA_DV, GLA_DK), BF16),
                        pltpu.VMEM((ts, GLA_VW), BF16),
                        pltpu.VMEM((GLA_HEADS, GLA_DV, GLA_DK), F32)],
        compiler_params=pltpu.CompilerParams(dimension_semantics=("arbitrary", "arbitrary"),
                                             vmem_limit_bytes=VMEM_LIMIT),
        name="mixer_b",
    )(x, m, g, w, wa2, ba, nrm, wout)


def _mixer_c_kernel(x_ref, m_ref, g_ref, w_hbm, cw_ref, cb_ref, dtb_ref, a_ref, dx_ref, nrm_ref,
                    wout_ref, wo_ref, o_ref,
                    w_ref, w_sem, h_scr, cbuf, xs_scr, bc_scr, z_scr, dt_scr, y_scr, yb_scr, st_ref, *, layer):
    C = SSD_CHUNK
    N = SSD_STATE
    P = SSD_HEADDIM
    ts = x_ref.shape[0]
    GN = SSD_GROUPS * N

    @pl.when((pl.program_id(0) == 0) & (pl.program_id(1) == 0))
    def _():
        cp = pltpu.make_async_copy(w_hbm.at[layer], w_ref, w_sem)
        cp.start()
        cp.wait()

    @pl.when(pl.program_id(1) == 0)
    def _():
        st_ref[...] = jnp.zeros_like(st_ref)
        cbuf[...] = jnp.zeros_like(cbuf)

    h_scr[...] = _rmsnorm(x_ref[...], g_ref[...]).astype(BF16)
    h = h_scr[...]

    CB = 256
    for zb in range(SSD_INNER // CB):
        z_scr[:, zb * CB:(zb + 1) * CB] = _silu(_dot(h, w_ref[:, zb * CB:(zb + 1) * CB]))
    o_gt = SSD_INNER + SSD_CONV_DIM
    o_dt = o_gt + D_MODEL
    dt_scr[...] = jax.nn.softplus(_dot(h, w_ref[:, o_dt:o_dt + LANES]) + dtb_ref[...])

    for cblk in range(SSD_CONV_DIM // CB):
        cs = slice(cblk * CB, (cblk + 1) * CB)
        proj = _dot(h, w_ref[:, SSD_INNER + cblk * CB:SSD_INNER + (cblk + 1) * CB])
        ext = jnp.concatenate([cbuf[:, cs], proj], axis=0)
        acc = cb_ref[:, cs] + proj * cw_ref[SSD_CONV - 1:SSD_CONV, cs]
        for k in range(1, SSD_CONV):
            j = SSD_CONV - 1 - k
            acc = acc + pltpu.roll(ext, k, 0)[CONV_PAD:] * cw_ref[j:j + 1, cs]
        act = _silu(acc)
        if cblk * CB < SSD_INNER:
            xs_scr[:, cs] = act
        else:
            bc_scr[:, cblk * CB - SSD_INNER:(cblk + 1) * CB - SSD_INNER] = act.astype(BF16)
        cbuf[:, cs] = proj[ts - CONV_PAD:]

    ii = lax.broadcasted_iota(I32, (C, C), 0)
    jj = lax.broadcasted_iota(I32, (C, C), 1)
    causal = ii >= jj
    tril = causal.astype(F32)
    triu = (ii <= jj).astype(F32)
    a_row = a_ref[...]
    first_head = lax.broadcasted_iota(I32, (1, 2 * P), 1) < P

    def chunk_body(c, carry):
        r0 = pl.multiple_of(c * C, C)
        rs = pl.ds(r0, C)
        dt = dt_scr[rs, :]
        dta = dt * a_row
        acs = _dot_hi(tril, dta)
        acs_t = _dot_tn_hi(dta, triu)
        last_row = acs[C - 1:C, :]
        dt_t = dt.T
        w1_t = (dt * jnp.exp(last_row - acs)).T
        dec = jnp.exp(last_row)
        for g in range(SSD_GROUPS):
            bcg = bc_scr[rs, g * N:(g + 1) * N]
            ccg = bc_scr[rs, GN + g * N:GN + (g + 1) * N]
            cb = _dot_nt(ccg, bcg)
            ccf = ccg.astype(F32)
            bct = bcg.astype(F32).T
            for rp in range(SSD_HEADS_PER_GROUP // 2):
                h0 = g * SSD_HEADS_PER_GROUP + 2 * rp
                pc = slice(h0 * P, (h0 + 2) * P)
                xpb = xs_scr[rs, pc].astype(BF16)
                stp = st_ref[h0 // 2]
                stpb = stp.astype(BF16)
                ys, news = [], []
                for hh in (h0, h0 + 1):
                    hc = slice(hh, hh + 1)
                    col = jnp.broadcast_to(acs[:, hc], (C, C))
                    lmat = jnp.exp(jnp.where(causal, col - acs_t[hc, :], -jnp.inf))
                    a1 = (lmat * cb * dt_t[hc, :]).astype(BF16)
                    a2 = (jnp.exp(col) * ccf).astype(BF16)
                    ys.append(_dot(a1, xpb) + _dot(a2, stpb))
                    news.append(_dot((bct * w1_t[hc, :]).astype(BF16), xpb))
                y_scr[rs, pc] = jnp.where(first_head, ys[0], ys[1])
                decp = jnp.where(first_head, dec[:, h0:h0 + 1], dec[:, h0 + 1:h0 + 2])
                st_ref[h0 // 2] = decp * stp + jnp.where(first_head, news[0], news[1])
        return carry

    lax.fori_loop(0, ts // C, chunk_body, 0)

    GW = SSD_INNER // SSD_GROUPS
    for g in range(SSD_GROUPS):
        gs = slice(g * GW, (g + 1) * GW)
        y = (y_scr[:, gs] + xs_scr[:, gs] * dx_ref[:, gs]) * z_scr[:, gs]
        y = y * lax.rsqrt(jnp.mean(y * y, axis=-1, keepdims=True) + EPS) * nrm_ref[:, gs]
        yb_scr[:, gs] = y.astype(BF16)
    yc = _dot(yb_scr[...], wout_ref[...])
    gate = jax.nn.sigmoid(_dot(h, w_ref[:, o_gt:o_gt + D_MODEL]))
    merged = m_ref[...] + yc * gate
    o_ref[...] = x_ref[...] + _dot(merged.astype(BF16), wo_ref[...])


def _mixer_c(x, m, g, w, cw, cb, dtb, a_row, dx, nrm, wout, wo, layer):
    B, S, D = x.shape
    ts = min(TS_C, S)
    consts = (g, w, cw, cb, dtb, a_row, dx, nrm, wout, wo)

    def spec(c):
        if c is w:
            return pl.BlockSpec(memory_space=pl.ANY)
        return _layer_spec(c, layer) if (c is wout or c is wo) else _const_spec(c.shape)

    return pl.pallas_call(
        functools.partial(_mixer_c_kernel, layer=layer),
        grid=(B, S // ts),
        in_specs=[_seq_spec(ts, D), _seq_spec(ts, D)] + [spec(c) for c in consts],
        out_specs=_seq_spec(ts, D),
        out_shape=jax.ShapeDtypeStruct((B, S, D), F32),
        scratch_shapes=[pltpu.VMEM(w.shape[1:], BF16), pltpu.SemaphoreType.DMA(()),
                        pltpu.VMEM((ts, D), BF16),
                        pltpu.VMEM((CONV_PAD, SSD_CONV_DIM), F32),
                        pltpu.VMEM((ts, SSD_INNER), F32),
                        pltpu.VMEM((ts, 2 * SSD_GROUPS * SSD_STATE), BF16),
                        pltpu.VMEM((ts, SSD_INNER), F32),
                        pltpu.VMEM((ts, LANES), F32),
                        pltpu.VMEM((ts, SSD_INNER), F32),
                        pltpu.VMEM((ts, SSD_INNER), BF16),
                        pltpu.VMEM((SSD_HEADS // 2, SSD_STATE, 2 * SSD_HEADDIM), F32)],
        compiler_params=pltpu.CompilerParams(dimension_semantics=("arbitrary", "arbitrary"),
                                             vmem_limit_bytes=VMEM_LIMIT),
        name="mixer_c",
    )(x, m, *consts)


ROUTER_ROWS = 8 + MOE_EXPERTS


def _router_kernel(x_ref, g_ref, wr_ref, br_ref, hf_ref, mi_ref, mf_ref, cnt_ref, cnt_scr):
    tm = x_ref.shape[0]
    E = MOE_EXPERTS
    NEG = -jnp.inf

    @pl.when(pl.program_id(0) == 0)
    def _():
        cnt_scr[...] = jnp.zeros_like(cnt_scr)

    hf = _rmsnorm(x_ref[...], g_ref[...])
    _store_token_tiles(hf_ref, hf)
    logits = _dot_nt_hi(wr_ref[...], hf) + br_ref[...]

    sub8 = lax.broadcasted_iota(I32, (8, tm), 0)
    gl = jnp.where(sub8 < MOE_GROUPS, logits[0:8], NEG)
    gmax = jnp.max(gl, axis=0, keepdims=True)
    g_sel = jnp.min(jnp.where(gl == gmax, sub8, 8), axis=0, keepdims=True)
    p_sel = 1.0 / jnp.sum(jnp.exp(gl - gmax), axis=0, keepdims=True)

    el = jnp.zeros((MOE_PER_GROUP, tm), F32)
    for g in range(MOE_GROUPS):
        el = jnp.where(g_sel == g, logits[8 + g * MOE_PER_GROUP:8 + (g + 1) * MOE_PER_GROUP], el)
    m1 = jnp.max(el, axis=0, keepdims=True)
    i1 = jnp.min(jnp.where(el == m1, sub8, 8), axis=0, keepdims=True)
    el2 = jnp.where(sub8 == i1, NEG, el)
    m2 = jnp.max(el2, axis=0, keepdims=True)
    i2 = jnp.min(jnp.where(el2 == m2, sub8, 8), axis=0, keepdims=True)
    e2 = jnp.exp(m2 - m1)
    w0 = p_sel / (1.0 + e2)
    w1 = p_sel * e2 / (1.0 + e2)
    eid0 = g_sel * MOE_PER_GROUP + i1
    eid1 = g_sel * MOE_PER_GROUP + i2

    subE = lax.broadcasted_iota(I32, (E, tm), 0)
    oh0 = subE == eid0
    oh1 = subE == eid1
    oh = (oh0 | oh1).astype(BF16)
    t_r = lax.broadcasted_iota(I32, (tm, tm), 0)
    t_c = lax.broadcasted_iota(I32, (tm, tm), 1)
    before = (t_r < t_c).astype(BF16)
    tot = _dot(oh, before) + cnt_scr[:, 0:1]
    rank0 = jnp.sum(jnp.where(oh0, tot, 0.0), axis=0, keepdims=True)
    rank1 = jnp.sum(jnp.where(oh1, tot, 0.0), axis=0, keepdims=True)
    cnt_new = cnt_scr[...] + jnp.sum(oh.astype(F32), axis=1, keepdims=True)
    cnt_scr[...] = cnt_new
    cnt_ref[...] = cnt_new.astype(I32)

    mi_ref[...] = jnp.zeros_like(mi_ref)
    mi_ref[0:1, :] = eid0
    mi_ref[1:2, :] = eid1
    mi_ref[2:3, :] = rank0.astype(I32)
    mi_ref[3:4, :] = rank1.astype(I32)
    mf_ref[...] = jnp.zeros_like(mf_ref)
    mf_ref[0:1, :] = w0
    mf_ref[1:2, :] = w1


def _router(x2, g, wr, br):
    T, D = x2.shape
    tm = min(TM_ROUTER, T)
    return pl.pallas_call(
        _router_kernel,
        grid=(T // tm,),
        in_specs=[pl.BlockSpec((tm, D), lambda i: (i, 0)), _const_spec(g.shape), _const_spec(wr.shape),
                  _const_spec(br.shape)],
        out_specs=[pl.BlockSpec((tm * ROW_TILE, LANES), lambda i: (i, 0)),
                   pl.BlockSpec((8, tm), lambda i: (0, i)),
                   pl.BlockSpec((8, tm), lambda i: (0, i)),
                   pl.BlockSpec((MOE_EXPERTS, LANES), lambda i: (0, 0))],
        out_shape=[jax.ShapeDtypeStruct((T * ROW_TILE, LANES), F32),
                   jax.ShapeDtypeStruct((8, T), I32),
                   jax.ShapeDtypeStruct((8, T), F32),
                   jax.ShapeDtypeStruct((MOE_EXPERTS, LANES), I32)],
        scratch_shapes=[pltpu.VMEM((MOE_EXPERTS, LANES), F32)],
        compiler_params=pltpu.CompilerParams(dimension_semantics=("arbitrary",),
                                             vmem_limit_bytes=VMEM_LIMIT),
        name="moe_router",
    )(x2, g, wr, br)


def _token_rows(ref, first, n):
    return ref.at[pl.ds(pl.multiple_of(first * ROW_TILE, ROW_TILE), n * ROW_TILE)]


def _row_copy(src_ref, src_row, dst_ref, dst_row, sem):
    return pltpu.make_async_copy(_token_rows(src_ref, src_row, 1), _token_rows(dst_ref, dst_row, 1), sem)


def _dest_kernel(ps_ref, mi_ref, o_ref):
    nt, _, tm2 = o_ref.shape
    tm = tm2 // 2
    eid = mi_ref[0:2, :]
    d = mi_ref[2:4, :]
    for e in range(MOE_EXPERTS):
        d = d + jnp.where(eid == e, ps_ref[e], 0)
    for j in range(nt):
        o_ref[j, :, 0:tm] = d[0:1, j * tm:(j + 1) * tm]
        o_ref[j, :, tm:tm2] = d[1:2, j * tm:(j + 1) * tm]


def _dest_slots(pstarts, mi, tm):
    T = mi.shape[1]
    nt = min(8, T // tm)
    return pl.pallas_call(
        _dest_kernel,
        grid_spec=pltpu.PrefetchScalarGridSpec(
            num_scalar_prefetch=1,
            grid=(T // (nt * tm),),
            in_specs=[pl.BlockSpec((8, nt * tm), lambda i, ps: (0, i))],
            out_specs=pl.BlockSpec((nt, 1, 2 * tm), lambda i, ps: (i, 0, 0))),
        out_shape=jax.ShapeDtypeStruct((T // tm, 1, 2 * tm), I32),
        compiler_params=pltpu.CompilerParams(dimension_semantics=("arbitrary",)),
        name="moe_dest",
    )(pstarts, mi)


def _dispatch_kernel(pends_ref, dest_ref, hf_ref, xbuf_ref, zero_scr, sem):
    tm = hf_ref.shape[0] // ROW_TILE
    bm = zero_scr.shape[0] // ROW_TILE
    n_blocks = xbuf_ref.shape[0] // (bm * ROW_TILE)

    @pl.when(pl.program_id(0) == 0)
    def _():
        zero_scr[...] = jnp.zeros_like(zero_scr)

        def last_block(e):
            return _token_rows(xbuf_ref, jnp.maximum(pends_ref[e] - bm, 0), bm)

        for e in range(MOE_EXPERTS):
            pltpu.make_async_copy(zero_scr, last_block(e), sem).start()
        for e in range(MOE_EXPERTS):
            pltpu.make_async_copy(zero_scr, last_block(e), sem).wait()

        def zero_start(b, c):
            pltpu.make_async_copy(zero_scr, _token_rows(xbuf_ref, b * bm, bm), sem).start()
            return c

        def zero_wait(b, c):
            pltpu.make_async_copy(zero_scr, _token_rows(xbuf_ref, b * bm, bm), sem).wait()
            return c

        first_unused = pends_ref[MOE_EXPERTS - 1] // bm
        lax.fori_loop(first_unused, n_blocks, zero_start, 0)
        lax.fori_loop(first_unused, n_blocks, zero_wait, 0)

    def issue(j, c):
        for u in range(MOVE_UNROLL):
            r = j * MOVE_UNROLL + u
            _row_copy(hf_ref, r, xbuf_ref, dest_ref[0, 0, r], sem).start(priority=0)
            _row_copy(hf_ref, r, xbuf_ref, dest_ref[0, 0, tm + r], sem).start(priority=1)
        return c

    lax.fori_loop(0, tm // MOVE_UNROLL, issue, 0)
    for _ in range(MOE_TOPK):
        pltpu.make_async_copy(hf_ref, _token_rows(xbuf_ref, 0, tm), sem).wait()


def _dispatch(hf, dest3, pends, n_rows):
    tm = dest3.shape[2] // 2
    T = dest3.shape[0] * tm
    return pl.pallas_call(
        _dispatch_kernel,
        grid_spec=pltpu.PrefetchScalarGridSpec(
            num_scalar_prefetch=1,
            grid=(T // tm,),
            in_specs=[pl.BlockSpec((1, 1, 2 * tm), lambda i, pe: (i, 0, 0), memory_space=pltpu.SMEM),
                      pl.BlockSpec((tm * ROW_TILE, LANES), lambda i, pe: (i, 0))],
            out_specs=pl.BlockSpec(memory_space=pl.ANY),
            scratch_shapes=[pltpu.VMEM((BM_EXPERT * ROW_TILE, LANES), F32), pltpu.SemaphoreType.DMA(())]),
        out_shape=jax.ShapeDtypeStruct((n_rows * ROW_TILE, LANES), F32),
        compiler_params=pltpu.CompilerParams(dimension_semantics=("arbitrary",),
                                             vmem_limit_bytes=VMEM_LIMIT),
        name="moe_dispatch",
    )(pends, dest3, hf)


def _expert_kernel(blk_e_ref, nb_ref, x_ref, wg_ref, wu_ref, wd_ref, y_ref, wg_b, wu_b, wd_b):
    i = pl.program_id(0)
    used = i < nb_ref[0]

    @pl.when(used & ((i == 0) | (blk_e_ref[i] != blk_e_ref[jnp.maximum(i - 1, 0)])))
    def _():
        wg_b[...] = wg_ref[...].astype(BF16)
        wu_b[...] = wu_ref[...].astype(BF16)
        wd_b[...] = wd_ref[...].astype(BF16)

    @pl.when(used)
    def _():
        xb = _load_token_tiles(x_ref, x_ref.shape[0] // ROW_TILE).astype(BF16)
        a = _dot(xb, wg_b[...])
        u = _dot(xb, wu_b[...])
        _store_token_tiles(y_ref, _dot((_silu(a) * u).astype(BF16), wd_b[...]))

    @pl.when(jnp.logical_not(used))
    def _():
        y_ref[...] = jnp.zeros_like(y_ref)


def _experts(xbuf, blk_e, nb_used, wg, wu, wd, layer):
    P = xbuf.shape[0] // ROW_TILE
    D = D_MODEL
    bm = BM_EXPERT
    FF = wg.shape[-1]

    def row_map(i, be, nb):
        return (i, 0)

    def used_row_map(i, be, nb):
        return (jnp.minimum(i, nb[0] - 1), 0)

    def w_map(i, be, nb):
        return (layer, be[jnp.minimum(i, nb[0] - 1)], 0, 0)

    return pl.pallas_call(
        _expert_kernel,
        grid_spec=pltpu.PrefetchScalarGridSpec(
            num_scalar_prefetch=2,
            grid=(P // bm,),
            in_specs=[pl.BlockSpec((bm * ROW_TILE, LANES), used_row_map),
                      pl.BlockSpec((None, None, D, FF), w_map),
                      pl.BlockSpec((None, None, D, FF), w_map),
                      pl.BlockSpec((None, None, FF, D), w_map)],
            out_specs=pl.BlockSpec((bm * ROW_TILE, LANES), row_map),
            scratch_shapes=[pltpu.VMEM((D, FF), BF16), pltpu.VMEM((D, FF), BF16), pltpu.VMEM((FF, D), BF16)]),
        out_shape=jax.ShapeDtypeStruct((P * ROW_TILE, LANES), F32),
        compiler_params=pltpu.CompilerParams(dimension_semantics=("arbitrary",),
                                             vmem_limit_bytes=VMEM_LIMIT),
        name="moe_experts",
    )(blk_e, nb_used, xbuf, wg, wu, wd)


def _combine_kernel(dcur_ref, dnext_ref, x_ref, w_ref, gfin_ref, ybuf_ref, o_ref, rbuf, sem, *, final_norm):
    i = pl.program_id(0)
    n = pl.num_programs(0)
    tm = x_ref.shape[0]
    slot = i % 2

    def issue(dref, s):
        def body(j, c):
            for u in range(MOVE_UNROLL):
                r = j * MOVE_UNROLL + u
                _row_copy(ybuf_ref, dref[0, 0, r], rbuf.at[s, 0], r, sem.at[s]).start(priority=0)
                _row_copy(ybuf_ref, dref[0, 0, tm + r], rbuf.at[s, 1], r, sem.at[s]).start(priority=1)
            return c

        lax.fori_loop(0, tm // MOVE_UNROLL, body, 0)

    @pl.when(i == 0)
    def _():
        issue(dcur_ref, 0)

    @pl.when(i + 1 < n)
    def _():
        issue(dnext_ref, 1 - slot)

    for k in range(MOE_TOPK):
        pltpu.make_async_copy(_token_rows(ybuf_ref, 0, tm), rbuf.at[slot, k], sem.at[slot]).wait()
    w = w_ref[...]
    r0 = _load_token_tiles(rbuf.at[slot, 0], tm)
    r1 = _load_token_tiles(rbuf.at[slot, 1], tm)
    out = x_ref[...] + (r0 * w[:, 0:1] + r1 * w[:, 1:2])
    if final_norm:
        out = _rmsnorm(out, gfin_ref[...])
    o_ref[...] = out


def _combine(x2, ybuf, dest3, wt, gfin, final_norm):
    T, D = x2.shape
    tm = dest3.shape[2] // 2
    n = T // tm
    return pl.pallas_call(
        functools.partial(_combine_kernel, final_norm=final_norm),
        grid=(n,),
        in_specs=[pl.BlockSpec((1, 1, 2 * tm), lambda i: (i, 0, 0), memory_space=pltpu.SMEM),
                  pl.BlockSpec((1, 1, 2 * tm), lambda i: (jnp.minimum(i + 1, n - 1), 0, 0),
                               memory_space=pltpu.SMEM),
                  pl.BlockSpec((tm, D), lambda i: (i, 0)),
                  pl.BlockSpec((tm, MOE_TOPK), lambda i: (i, 0)),
                  _const_spec(gfin.shape),
                  pl.BlockSpec(memory_space=pl.ANY)],
        out_specs=pl.BlockSpec((tm, D), lambda i: (i, 0)),
        out_shape=jax.ShapeDtypeStruct((T, D), F32),
        scratch_shapes=[pltpu.VMEM((2, MOE_TOPK, tm * ROW_TILE, LANES), F32), pltpu.SemaphoreType.DMA((2,))],
        compiler_params=pltpu.CompilerParams(dimension_semantics=("arbitrary",),
                                             vmem_limit_bytes=VMEM_LIMIT),
        name="moe_combine",
    )(dest3, dest3, x2, wt, gfin, ybuf)


def _moe(x2, g_ffn, wr, br, wg, wu, wd, layer, gfin, final_norm):
    T, D = x2.shape
    E = MOE_EXPERTS
    bm = BM_EXPERT
    A = T * MOE_TOPK
    n_blocks = -(-A // bm) + E
    hf, mi, mf, cnt = _router(x2, g_ffn, wr, br)
    counts = cnt[:, 0]
    pcounts = (counts + bm - 1) // bm * bm
    pends = jnp.cumsum(pcounts)
    pstarts = pends - pcounts
    blk_start = jnp.arange(n_blocks, dtype=I32)[:, None] * bm
    blk_e = jnp.minimum(jnp.sum((pends[None, :] <= blk_start).astype(I32), axis=1), E - 1)
    nb_used = jnp.maximum(pends[-1:] // bm, 1).astype(I32)
    tm = min(TM_MOVE, T)
    dest3 = _dest_slots(pstarts.astype(I32), mi, tm)
    xbuf = _dispatch(hf, dest3, pends.astype(I32), n_blocks * bm)
    ybuf = _experts(xbuf, blk_e, nb_used, wg, wu, wd, layer)
    return _combine(x2, ybuf, dest3, mf[0:2].T, gfin, final_norm)


def kernel(x, positions, g_mix, w_in, ret_norm, w_ret_out, gla_w_a2, gla_b_a, gla_norm, w_gla_out, ssd_conv_w, ssd_conv_b, ssd_dt_bias, ssd_a_log, ssd_d, ssd_norm, w_ssd_out, w_o, g_ffn, w_rg, b_rg, w_re, b_re, w_exp_gate, w_exp_up, w_exp_down, g_final):
    B, S, D = x.shape
    L = w_in.shape[0]
    T = B * S

    def cols(a, b):
        return w_in[:, :, a:b]

    def padc(w, n):
        return jnp.pad(w, ((0, 0), (0, 0), (0, n - w.shape[-1])))

    w_a = jnp.concatenate([cols(O_RQ, O_GQ), cols(O_GTA, O_GTB)], axis=-1).astype(BF16)
    w_b = jnp.concatenate([cols(O_GQ, O_GA), cols(O_GTB, O_GTC), padc(cols(O_GA, O_SZ), LANES)],
                          axis=-1).astype(BF16)
    w_c = jnp.concatenate([cols(O_SZ, O_SDT), cols(O_GTC, O_GTC + D_MODEL), padc(cols(O_SDT, O_GTA), LANES)],
                          axis=-1).astype(BF16)
    wa2 = jnp.pad(gla_w_a2, ((0, 0), (0, LANES - GLA_RANK), (0, 0))).astype(BF16)
    padl = lambda v: jnp.pad(v, ((0, 0), (0, LANES - v.shape[-1])))[:, None, :]
    dtb = padl(ssd_dt_bias.astype(F32))
    a_row = padl(-jnp.exp(ssd_a_log.astype(F32)))
    dx = jnp.repeat(ssd_d.astype(F32), SSD_HEADDIM, axis=-1)[:, None, :]
    half = RET_DK // 2
    inv = ROPE_BASE ** (-jnp.arange(half, dtype=F32) / half)
    inv2 = jnp.concatenate([inv, inv])[None, :]
    posf = positions.astype(F32)[:, :, None]
    wr = jnp.concatenate([jnp.pad(jnp.swapaxes(w_rg, 1, 2), ((0, 0), (0, 8 - MOE_GROUPS), (0, 0))),
                          jnp.swapaxes(w_re, 1, 2)], axis=1).astype(F32)
    br = jnp.concatenate([jnp.pad(b_rg, ((0, 0), (0, 8 - MOE_GROUPS))), b_re], axis=1).astype(F32)[:, :, None]
    row = lambda v: v[:, None, :].astype(F32)
    g_mix_r, g_ffn_r = row(g_mix), row(g_ffn)
    ret_norm_r, gla_norm_r, ssd_norm_r = row(ret_norm), row(gla_norm), row(ssd_norm)
    gla_ba_r, conv_b_r = row(gla_b_a), row(ssd_conv_b)
    w_ret_o, w_gla_o = w_ret_out.astype(BF16), w_gla_out.astype(BF16)
    w_ssd_o, w_o_b = w_ssd_out.astype(BF16), w_o.astype(BF16)
    gfin = g_final[None, :].astype(F32)

    cos2, sin2 = _rope_tables(posf, inv2)
    for l in range(L):
        m = _mixer_a(x, cos2, sin2, g_mix_r[l], w_a, ret_norm_r[l], w_ret_o, l)
        m = _mixer_b(x, m, g_mix_r[l], w_b, wa2[l], gla_ba_r[l], gla_norm_r[l], w_gla_o, l)
        x = _mixer_c(x, m, g_mix_r[l], w_c, ssd_conv_w[l].astype(F32), conv_b_r[l], dtb[l], a_row[l],
                     dx[l], ssd_norm_r[l], w_ssd_o, w_o_b, l)
        x = _moe(x.reshape(T, D), g_ffn_r[l], wr[l], br[l], w_exp_gate, w_exp_up, w_exp_down, l,
                 gfin, l == L - 1).reshape(B, S, D)
    return x
```
